```python
import math
import jax, jax.numpy as jnp
from jax import lax
import numpy as np

D_MODEL = 1024
BATCH = 16
SEQ = 2048
DEPTH = 4

N_MEM = 256
EPS = 1e-6
A_HEADS = 8
A_HEAD_DIM = 64
A_WIDTH = A_HEADS * A_HEAD_DIM
A_PATTERNS = ((128, 1), (512, 4), (2048, 16))
ROPE_THETA = 10000.0
B_HEADS = 4
B_DK = 64
B_DV = 128
B_KW = B_HEADS * B_DK
B_VW = B_HEADS * B_DV
B_RANK = 16
B_TAU = 16.0
B_CHUNK = 64
C_WIDTH = 512
C_GROUP = 16
C_NGROUPS = C_WIDTH // C_GROUP
C_STATE = 64
D_WIDTH = 512
D_KERNEL = 31
X_HEADS = 4
X_HEAD_DIM = D_MODEL // X_HEADS
D_FF = 2816
FFN_KERNEL = 3
N_EVEN = (DEPTH + 1) // 2
N_ODD = DEPTH // 2
IN_AB = 3 * A_WIDTH + 2 * B_KW + 2 * B_VW + 2 * B_RANK
IN_CD = C_WIDTH + 2 * D_WIDTH

kernel_name = "hybrid_dilated_gla_s5_conformer_encoder"


def split_sizes(t, sizes):
    idx = np.cumsum(sizes)[:-1].tolist()
    return jnp.split(t, idx, axis=-1)


def rmsnorm(x, g):
    xf = x.astype(jnp.float32)
    r = lax.rsqrt(jnp.mean(xf * xf, -1, keepdims=True) + EPS)
    return (xf * r).astype(x.dtype) * g


def layernorm(x, g, b):
    xf = x.astype(jnp.float32)
    mu = jnp.mean(xf, -1, keepdims=True)
    var = jnp.mean(jnp.square(xf - mu), -1, keepdims=True)
    return ((xf - mu) * lax.rsqrt(var + EPS)).astype(x.dtype) * g + b


def rope(x, cos, sin):
    x1, x2 = jnp.split(x, 2, axis=-1)
    return jnp.concatenate([x1 * cos - x2 * sin, x2 * cos + x1 * sin], axis=-1)


def depthwise_conv(u, w, b):
    K = w.shape[0]
    out = lax.conv_general_dilated(
        u, w[:, None, :], window_strides=(1,), padding=[((K - 1) // 2, K // 2)],
        dimension_numbers=('NWC', 'WIO', 'NWC'), feature_group_count=u.shape[-1])
    return out + b


def dilated_window_attn(q, k, v, dilation, side):
    Bn, S, H, hd = q.shape
    L = S // dilation
    nb = -(-L // side)
    Lp = nb * side

    def to_blocks(t):
        t = t.reshape(Bn, L, dilation, H, hd)
        t = jnp.pad(t, ((0, 0), (0, Lp - L), (0, 0), (0, 0), (0, 0)))
        return t.reshape(Bn, nb, side, dilation, H, hd)

    def neighbours(t):
        tp = jnp.pad(t, ((0, 0), (1, 1), (0, 0), (0, 0), (0, 0), (0, 0)))
        return jnp.concatenate([tp[:, :-2], tp[:, 1:-1], tp[:, 2:]], axis=2)

    qb = to_blocks(q)
    kw = neighbours(to_blocks(k))
    vw = neighbours(to_blocks(v))
    s = jnp.einsum('bnqrhd,bnkrhd->bnrhqk', qb, kw).astype(jnp.float32) * (hd ** -0.5)
    q_idx = jnp.arange(nb)[:, None] * side + jnp.arange(side)[None, :]
    k_idx = (jnp.arange(nb)[:, None] - 1) * side + jnp.arange(3 * side)[None, :]
    rel = k_idx[:, None, :] - q_idx[:, :, None]
    valid = (jnp.abs(rel) <= side) & (k_idx[:, None, :] >= 0) & (k_idx[:, None, :] < L)
    s = jnp.where(valid[None, :, None, None], s, -1e30)
    m = jnp.max(s, -1, keepdims=True)
    p = jnp.exp(s - m)
    den = jnp.sum(p, -1, keepdims=True)
    o = jnp.einsum('bnrhqk,bnkrhd->bnqrhd', (p / den).astype(v.dtype), vw)
    lse = (m + jnp.log(den))[..., 0]
    o = o.reshape(Bn, Lp, dilation, H, hd)[:, :L].reshape(Bn, S, H, hd)
    lse = jnp.transpose(lse, (0, 1, 4, 2, 3)).reshape(Bn, Lp, dilation, H)[:, :L].reshape(Bn, S, H)
    return o, lse


def dilated_mixture_attention(q, k, v):
    outs, lses = [], []
    for window, dilation in A_PATTERNS:
        o, l = dilated_window_attn(q, k, v, dilation, window // (2 * dilation))
        outs.append(o)
        lses.append(l)
    w = jax.nn.softmax(jnp.stack(lses, 0), axis=0)
    return jnp.einsum('gbsh,gbshd->bshd', w.astype(q.dtype), jnp.stack(outs, 0))


def gla_chunked(q, k, v, logg, include_diag):
    Bn, H, S, dk = q.shape
    dv = v.shape[-1]
    C = B_CHUNK
    N = S // C
    qc = q.reshape(Bn, H, N, C, dk)
    kc = k.reshape(Bn, H, N, C, dk)
    vc = v.reshape(Bn, H, N, C, dv)
    G = jnp.cumsum(logg.reshape(Bn, H, N, C, dk), axis=3)
    Gtot = G[:, :, :, -1:, :]
    q_in = qc * jnp.exp(G)
    k_in = kc * jnp.exp(-G)
    att = jnp.einsum('bhnik,bhnjk->bhnij', q_in, k_in)
    mask = jnp.tril(jnp.ones((C, C), dtype=bool), 0 if include_diag else -1)
    att = jnp.where(mask, att, jnp.zeros_like(att))
    o_intra = jnp.einsum('bhnij,bhnjv->bhniv', att, vc)
    U = jnp.einsum('bhnjk,bhnjv->bhnkv', kc * jnp.exp(Gtot - G), vc)
    a = jnp.exp(Gtot[:, :, :, 0, :])

    def step(s_prev, inp):
        a_n, u_n = inp
        return a_n[..., None] * s_prev + u_n, s_prev

    s0 = jnp.zeros((Bn, H, dk, dv), dtype=U.dtype)
    _, s_before = lax.scan(step, s0, (jnp.moveaxis(a, 2, 0), jnp.moveaxis(U, 2, 0)))
    s_before = jnp.moveaxis(s_before, 0, 2)
    o_inter = jnp.einsum('bhnik,bhnkv->bhniv', q_in, s_before)
    return (o_intra + o_inter).reshape(Bn, H, S, dv)


def head_rmsnorm(o, g):
    Bn, H, S, dv = o.shape
    of = o.astype(jnp.float32)
    of = of * lax.rsqrt(jnp.mean(of * of, -1, keepdims=True) + EPS)
    return of.astype(o.dtype).transpose(0, 2, 1, 3).reshape(Bn, S, H * dv) * g


def mixer_ab(hn, cos, sin, w_in, w_out, wg2, bg, g_norm):
    Bn, S, _ = hn.shape
    q_a, k_a, v_a, q_b, k_b, v_b, r_b, z_f, z_b = split_sizes(
        hn @ w_in, (A_WIDTH, A_WIDTH, A_WIDTH, B_KW, B_KW, B_VW, B_VW, B_RANK, B_RANK))
    heads_a = lambda t: t.reshape(Bn, S, A_HEADS, A_HEAD_DIM)
    qa = rope(heads_a(q_a), cos, sin)
    ka = rope(heads_a(k_a), cos, sin)
    o_a = dilated_mixture_attention(qa, ka, heads_a(v_a)).reshape(Bn, S, A_WIDTH)
    heads_b = lambda t, d: t.reshape(Bn, S, B_HEADS, d).transpose(0, 2, 1, 3)
    qb = heads_b(q_b, B_DK) * (B_DK ** -0.5)
    kb = heads_b(k_b, B_DK)
    vb = heads_b(v_b, B_DV)

    def log_gate(z, w2, b):
        logit = (z @ w2 + b).astype(jnp.float32)
        return heads_b((jax.nn.log_sigmoid(logit) / B_TAU).astype(hn.dtype), B_DK)

    g_f = log_gate(z_f, wg2[0], bg[0])
    g_b = log_gate(z_b, wg2[1], bg[1])
    flip = lambda t: jnp.flip(t, axis=2)
    o_b = gla_chunked(qb, kb, vb, g_f, True) + flip(
        gla_chunked(flip(qb), flip(kb), flip(vb), flip(g_b), False))
    o_b = head_rmsnorm(o_b.astype(hn.dtype), g_norm) * jax.nn.silu(r_b)
    return jnp.concatenate([o_a, o_b], axis=-1) @ w_out


def s5_scan(u, lam_re, lam_im, log_dt, b_re, b_im, c_re, c_im):
    f32 = jnp.float32
    S = u.shape[1]
    dt = jnp.exp(log_dt.astype(f32))[:, None]
    lr, li = lam_re.astype(f32), lam_im.astype(f32)
    mag = jnp.exp(lr * dt)
    ab_re, ab_im = mag * jnp.cos(li * dt), mag * jnp.sin(li * dt)
    den = lr * lr + li * li
    nr = ab_re - 1.0
    f_re = (nr * lr + ab_im * li) / den
    f_im = (ab_im * lr - nr * li) / den
    br, bi = b_re.astype(f32), b_im.astype(f32)
    bb_re = (f_re[..., None] * br - f_im[..., None] * bi).astype(u.dtype)
    bb_im = (f_re[..., None] * bi + f_im[..., None] * br).astype(u.dtype)
    bu_re = jnp.einsum('gph,bsgh->bsgp', bb_re, u)
    bu_im = jnp.einsum('gph,bsgh->bsgp', bb_im, u)
    G, P = lam_re.shape
    a_re = jnp.broadcast_to(ab_re.astype(u.dtype)[None, None], (1, S, G, P))
    a_im = jnp.broadcast_to(ab_im.astype(u.dtype)[None, None], (1, S, G, P))

    def combine(e1, e2):
        a1r, a1i, b1r, b1i = e1
        a2r, a2i, b2r, b2i = e2
        return (a1r * a2r - a1i * a2i, a1r * a2i + a1i * a2r,
                a2r * b1r - a2i * b1i + b2r, a2r * b1i + a2i * b1r + b2i)

    _, _, s_re, s_im = lax.associative_scan(combine, (a_re, a_im, bu_re, bu_im), axis=1)
    return (jnp.einsum('ghp,bsgp->bsgh', c_re, s_re)
            - jnp.einsum('ghp,bsgp->bsgh', c_im, s_im))


def conformer_conv(h, conv_w, conv_b, ln_g, ln_b):
    val, gate = jnp.split(h, 2, axis=-1)
    u = depthwise_conv(val * jax.nn.sigmoid(gate), conv_w, conv_b)
    return jax.nn.silu(layernorm(u, ln_g, ln_b))


def mixer_cd(hn, w_in, w_out, lam_re, lam_im, log_dt, b_re, b_im, c_re, c_im, d_skip,
             w_glu, b_glu, conv_w, conv_b, ln_g, ln_b):
    Bn, S, _ = hn.shape
    u_c, h_d = split_sizes(hn @ w_in, (C_WIDTH, 2 * D_WIDTH))
    u = u_c.reshape(Bn, S, C_NGROUPS, C_GROUP)
    y_f = s5_scan(u, lam_re[0], lam_im[0], log_dt[0], b_re[0], b_im[0], c_re[0], c_im[0])
    y_b = jnp.flip(s5_scan(jnp.flip(u, 1), lam_re[1], lam_im[1], log_dt[1],
                           b_re[1], b_im[1], c_re[1], c_im[1]), 1)
    y = (y_f + y_b).reshape(Bn, S, C_WIDTH) + d_skip * u_c
    z = jax.nn.gelu(y)
    o_c = z * jax.nn.sigmoid(z @ w_glu + b_glu)
    o_d = conformer_conv(h_d, conv_w, conv_b, ln_g, ln_b)
    return jnp.concatenate([o_c, o_d], axis=-1) @ w_out


def cross_attention(xn, memn, wq, wkv, wo):
    Bn, S, _ = xn.shape
    M = memn.shape[1]
    q = (xn @ wq).reshape(Bn, S, X_HEADS, X_HEAD_DIM)
    k, v = jnp.split(memn @ wkv, 2, axis=-1)
    k = k.reshape(Bn, M, X_HEADS, X_HEAD_DIM)
    v = v.reshape(Bn, M, X_HEADS, X_HEAD_DIM)
    s = jnp.einsum('bshd,bmhd->bhsm', q, k).astype(jnp.float32) * (X_HEAD_DIM ** -0.5)
    p = jax.nn.softmax(s, axis=-1).astype(v.dtype)
    o = jnp.einsum('bhsm,bmhd->bshd', p, v).reshape(Bn, S, D_MODEL)
    return o @ wo


def conv_ffn(hn, w_up, w_conv, b_conv, w_down):
    u = depthwise_conv(hn @ w_up, w_conv, b_conv)
    val, gate = jnp.split(u, 2, axis=-1)
    return (jax.nn.silu(gate) * val) @ w_down


def setup_inputs(seed: int = 0) -> dict:
    key = jax.random.key(seed)
    ks = jax.random.split(key, 40)
    nrm = lambda k, shape, scale: jax.random.normal(k, shape, jnp.float32) * scale
    gain = lambda k, shape: 1.0 + 0.02 * jax.random.normal(k, shape, jnp.float32)
    D, F = D_MODEL, D_FF
    offsets = jax.random.randint(ks[2], (BATCH, 1), 0, 8192, dtype=jnp.int32)
    positions = (jnp.arange(SEQ, dtype=jnp.int32)[None, :] + offsets).astype(jnp.int32)
    log_dt_lo, log_dt_hi = math.log(1e-3), math.log(1e-1)
    lam_im0 = math.pi * jnp.arange(C_STATE, dtype=jnp.float32)
    return {
        "x": nrm(ks[0], (BATCH, SEQ, D), 1.0),
        "mem": nrm(ks[1], (BATCH, N_MEM, D), 1.0),
        "positions": positions,
        "g_mix": gain(ks[3], (DEPTH, D)),
        "g_xattn": gain(ks[4], (DEPTH, D)),
        "g_mem": gain(ks[5], (DEPTH, D)),
        "w_xq": nrm(ks[6], (DEPTH, D, D), D ** -0.5),
        "w_xkv": nrm(ks[7], (DEPTH, D, 2 * D), D ** -0.5),
        "w_xo": nrm(ks[8], (DEPTH, D, D), D ** -0.5),
        "g_ffn": gain(ks[9], (DEPTH, D)),
        "w_up": nrm(ks[10], (DEPTH, D, 2 * F), D ** -0.5),
        "w_conv_ffn": nrm(ks[11], (DEPTH, FFN_KERNEL, 2 * F), FFN_KERNEL ** -0.5),
        "b_conv_ffn": nrm(ks[12], (DEPTH, 2 * F), 0.01),
        "w_down": nrm(ks[13], (DEPTH, F, D), F ** -0.5),
        "w_in_ab": nrm(ks[14], (N_EVEN, D, IN_AB), D ** -0.5),
        "w_out_ab": nrm(ks[15], (N_EVEN, D, D), D ** -0.5),
        "gla_wg2": nrm(ks[16], (N_EVEN, 2, B_RANK, B_KW), B_RANK ** -0.5),
        "gla_bg": nrm(ks[17], (N_EVEN, 2, B_KW), 0.1),
        "gla_norm": gain(ks[18], (N_EVEN, B_VW)),
        "w_in_cd": nrm(ks[19], (N_ODD, D, IN_CD), D ** -0.5),
        "w_out_cd": nrm(ks[20], (N_ODD, D, D), D ** -0.5),
        "s5_lam_re": -0.5 + nrm(ks[21], (N_ODD, 2, C_NGROUPS, C_STATE), 0.01),
        "s5_lam_im": lam_im0 + nrm(ks[22], (N_ODD, 2, C_NGROUPS, C_STATE), 0.01),
        "s5_log_dt": log_dt_lo + (log_dt_hi - log_dt_lo) * jax.random.uniform(
            ks[23], (N_ODD, 2, C_NGROUPS), jnp.float32),
        "s5_b_re": nrm(ks[24], (N_ODD, 2, C_NGROUPS, C_STATE, C_GROUP), (2.0 * C_GROUP) ** -0.5),
        "s5_b_im": nrm(ks[25], (N_ODD, 2, C_NGROUPS, C_STATE, C_GROUP), (2.0 * C_GROUP) ** -0.5),
        "s5_c_re": nrm(ks[26], (N_ODD, 2, C_NGROUPS, C_GROUP, C_STATE), (2.0 / C_STATE) ** 0.5),
        "s5_c_im": nrm(ks[27], (N_ODD, 2, C_NGROUPS, C_GROUP, C_STATE), (2.0 / C_STATE) ** 0.5),
        "s5_d": nrm(ks[28], (N_ODD, C_WIDTH), 1.0),
        "s5_w_glu": nrm(ks[29], (N_ODD, C_WIDTH, C_WIDTH), C_WIDTH ** -0.5),
        "s5_b_glu": nrm(ks[30], (N_ODD, C_WIDTH), 0.01),
        "conv_w": nrm(ks[31], (N_ODD, D_KERNEL, D_WIDTH), D_KERNEL ** -0.5),
        "conv_b": nrm(ks[32], (N_ODD, D_WIDTH), 0.01),
        "conv_ln_g": gain(ks[33], (N_ODD, D_WIDTH)),
        "conv_ln_b": nrm(ks[34], (N_ODD, D_WIDTH), 0.01),
        "g_final": gain(ks[35], (D,)),
    }


def reference(x, mem, positions, g_mix, g_xattn, g_mem, w_xq, w_xkv, w_xo, g_ffn, w_up,
              w_conv_ffn, b_conv_ffn, w_down, w_in_ab, w_out_ab, gla_wg2, gla_bg, gla_norm,
              w_in_cd, w_out_cd, s5_lam_re, s5_lam_im, s5_log_dt, s5_b_re, s5_b_im, s5_c_re,
              s5_c_im, s5_d, s5_w_glu, s5_b_glu, conv_w, conv_b, conv_ln_g, conv_ln_b, g_final):
    inv_freq = ROPE_THETA ** (-jnp.arange(0, A_HEAD_DIM, 2, dtype=jnp.float32) / A_HEAD_DIM)
    ang = positions.astype(jnp.float32)[..., None] * inv_freq
    cos = jnp.cos(ang)[:, :, None, :].astype(x.dtype)
    sin = jnp.sin(ang)[:, :, None, :].astype(x.dtype)
    h = x
    for layer in range(DEPTH):
        i = layer // 2
        hn = rmsnorm(h, g_mix[layer])
        if layer % 2 == 0:
            h = h + mixer_ab(hn, cos, sin, w_in_ab[i], w_out_ab[i], gla_wg2[i], gla_bg[i],
                             gla_norm[i])
        else:
            h = h + mixer_cd(hn, w_in_cd[i], w_out_cd[i], s5_lam_re[i], s5_lam_im[i],
                             s5_log_dt[i], s5_b_re[i], s5_b_im[i], s5_c_re[i], s5_c_im[i],
                             s5_d[i], s5_w_glu[i], s5_b_glu[i], conv_w[i], conv_b[i],
                             conv_ln_g[i], conv_ln_b[i])
        memn = rmsnorm(mem, g_mem[layer])
        h = h + cross_attention(rmsnorm(h, g_xattn[layer]), memn, w_xq[layer], w_xkv[layer],
                                w_xo[layer])
        h = h + conv_ffn(rmsnorm(h, g_ffn[layer]), w_up[layer], w_conv_ffn[layer],
                         b_conv_ffn[layer], w_down[layer])
    return rmsnorm(h, g_final)
```

```python
import functools
import math

import jax
import jax.numpy as jnp
import numpy as np
from jax import lax
from jax.experimental import pallas as pl
from jax.experimental.pallas import tpu as pltpu

F32 = jnp.float32
BF16 = jnp.bfloat16
EPS = 1e-6

LANES = 128
SUBLANES = 8
VMEM_LIMIT_BYTES = 52 * 1024 * 1024

A_HEADS, A_HEAD_DIM = 8, 64
A_PATTERNS = ((128, 1), (512, 4), (2048, 16))
ROPE_THETA = 10000.0
B_HEADS, B_DK, B_DV = 4, 64, 128
B_RANK, B_TAU, B_CHUNK = 16, 16.0, 64
C_GROUP, C_NGROUPS, C_STATE = 16, 32, 64
S5_CHUNK = 16
D_KERNEL = 31
X_HEADS = 4
FFN_KERNEL = 3
NEG_BIG = -1e30

ROW_TILE = 512
CONV_HALO = 16
FFN_HALO = 8


def _params(*sem):
    return pltpu.CompilerParams(dimension_semantics=sem, vmem_limit_bytes=VMEM_LIMIT_BYTES)


def _const_spec(shape):
    zeros = (0,) * len(shape)
    return pl.BlockSpec(shape, lambda *_: zeros, pipeline_mode=pl.Buffered(1))


def _row_spec(tm, width):
    return pl.BlockSpec((tm, width), lambda i: (i, 0))


def _dot(a, b):
    return jnp.dot(a, b, preferred_element_type=F32)


def _dot_nt(a, b):
    return lax.dot_general(a, b, (((1,), (1,)), ((), ())), preferred_element_type=F32)


def _dot_tn(a, b):
    return lax.dot_general(a, b, (((0,), (0,)), ((), ())), preferred_element_type=F32)


def _rmsnorm(x, g):
    return x * lax.rsqrt(jnp.mean(x * x, axis=-1, keepdims=True) + EPS) * g


def _sigmoid(x):
    return 1.0 / (1.0 + jnp.exp(-x))


def _silu(x):
    return x * _sigmoid(x)


def _rope_table_kernel(pos_ref, invf_ref, cos_ref, sin_ref):
    ang = pos_ref[...].astype(F32) * invf_ref[...]
    lane = lax.broadcasted_iota(jnp.int32, (1, LANES), 1)
    sign = jnp.where((lane % A_HEAD_DIM) < A_HEAD_DIM // 2, -1.0, 1.0)
    cos_ref[...] = jnp.cos(ang)
    sin_ref[...] = jnp.sin(ang) * sign


def rope_tables(positions):
    T = positions.size
    tm = 1024
    inv_freq = ROPE_THETA ** (-jnp.arange(0, A_HEAD_DIM, 2, dtype=F32) / A_HEAD_DIM)
    invf = jnp.tile(inv_freq, LANES // (A_HEAD_DIM // 2))[None, :]
    return pl.pallas_call(
        _rope_table_kernel,
        grid=(T // tm,),
        in_specs=[_row_spec(tm, 1), _const_spec((1, LANES))],
        out_specs=[_row_spec(tm, LANES), _row_spec(tm, LANES)],
        out_shape=[jax.ShapeDtypeStruct((T, LANES), F32)] * 2,
        compiler_params=_params("parallel"),
        name="rope_tables",
    )(positions.reshape(T, 1), invf)


def _in_ab_kernel(x_ref, g_ref, cos_ref, sin_ref, wqk_ref, wva_ref, wqkb_ref, wvr_ref, wz_ref,
                  qa_ref, ka_ref, va_ref, qb_ref, kb_ref, vb_ref, rb_ref, z_ref):
    xn = _rmsnorm(x_ref[...], g_ref[...]).astype(BF16)
    qk = _dot(xn, wqk_ref[...])
    width = qk.shape[1]
    reps = width // LANES
    cos = jnp.concatenate([cos_ref[...]] * reps, axis=1)
    sin = jnp.concatenate([sin_ref[...]] * reps, axis=1)
    half = A_HEAD_DIM // 2
    lane = lax.broadcasted_iota(jnp.int32, (1, width), 1)
    first_half = (lane % A_HEAD_DIM) < half
    partner = jnp.where(first_half, pltpu.roll(qk, width - half, axis=1), pltpu.roll(qk, half, axis=1))
    roped = qk * cos + partner * sin
    aw = width // 2
    qa_ref[...] = (roped[:, :aw] * (A_HEAD_DIM ** -0.5)).astype(BF16)
    ka_ref[...] = roped[:, aw:].astype(BF16)
    va_ref[...] = _dot(xn, wva_ref[...]).astype(BF16)
    qkb = _dot(xn, wqkb_ref[...])
    kw = qkb.shape[1] // 2
    qb_ref[...] = qkb[:, :kw] * (B_DK ** -0.5)
    kb_ref[...] = qkb[:, kw:]
    vr = _dot(xn, wvr_ref[...])
    vw = vr.shape[1] // 2
    vb_ref[...] = vr[:, :vw].astype(BF16)
    rb_ref[...] = vr[:, vw:].astype(BF16)
    z_ref[...] = _dot(xn, wz_ref[...])


def in_proj_ab(h, g, cos, sin, w_in):
    T, D = h.shape
    aw, kw, vw = A_HEADS * A_HEAD_DIM, B_HEADS * B_DK, B_HEADS * B_DV
    o = np.cumsum([0, aw, aw, aw, kw, kw, vw, vw, 2 * B_RANK])
    wb = w_in.astype(BF16)
    wqk, wva, wqkb, wvr, wz = (wb[:, o[0]:o[2]], wb[:, o[2]:o[3]], wb[:, o[3]:o[5]],
                               wb[:, o[5]:o[7]], wb[:, o[7]:o[8]])
    tm = ROW_TILE
    outs = [(aw, BF16), (aw, BF16), (aw, BF16), (kw, F32), (kw, F32), (vw, BF16), (vw, BF16),
            (2 * B_RANK, F32)]
    return pl.pallas_call(
        _in_ab_kernel,
        grid=(T // tm,),
        in_specs=[_row_spec(tm, D), _const_spec((1, D)), _row_spec(tm, LANES), _row_spec(tm, LANES),
                  _const_spec(wqk.shape), _const_spec(wva.shape), _const_spec(wqkb.shape),
                  _const_spec(wvr.shape), _const_spec(wz.shape)],
        out_specs=[_row_spec(tm, w) for w, _ in outs],
        out_shape=[jax.ShapeDtypeStruct((T, w), dt) for w, dt in outs],
        compiler_params=_params("parallel"),
        name="in_proj_ab",
    )(h, g[None, :], cos, sin, wqk, wva, wqkb, wvr, wz)


def _attn_kernel(q_ref, k_ref, v_ref, bias_ref, o_ref):
    q, k, v = q_ref[0], k_ref[0], v_ref[0]
    bias = bias_ref[...]
    lane = lax.broadcasted_iota(jnp.int32, (1, LANES), 1)
    outs = []
    for hh in range(LANES // A_HEAD_DIM):
        qh = jnp.where((lane // A_HEAD_DIM) == hh, q, jnp.zeros_like(q))
        s = _dot_nt(qh, k) + bias
        p = jnp.exp(s - jnp.max(s, axis=-1, keepdims=True))
        den = jnp.sum(p, axis=-1, keepdims=True)
        outs.append(_dot(p.astype(BF16), v) / den)
    o_ref[0] = jnp.where(lane < A_HEAD_DIM, outs[0], outs[1]).astype(o_ref.dtype)


def _mixture_bias(S):
    d = jnp.arange(S, dtype=jnp.int32)[None, :] - jnp.arange(S, dtype=jnp.int32)[:, None]
    count = jnp.zeros((S, S), F32)
    for window, dil in A_PATTERNS:
        side = window // (2 * dil)
        count = count + ((d % dil == 0) & (jnp.abs(d) <= side * dil)).astype(F32)
    return jnp.where(count > 0, jnp.log(jnp.maximum(count, 1.0)), NEG_BIG)


def dilated_attention(qa, ka, va, B, S):
    W = qa.shape[-1]
    tq = 256
    q3, k3, v3 = (t.reshape(B, S, W) for t in (qa, ka, va))
    bias = _mixture_bias(S)
    out = pl.pallas_call(
        _attn_kernel,
        grid=(S // tq, B, W // LANES),
        in_specs=[pl.BlockSpec((1, tq, LANES), lambda i, b, p: (b, i, p)),
                  pl.BlockSpec((1, S, LANES), lambda i, b, p: (b, 0, p)),
                  pl.BlockSpec((1, S, LANES), lambda i, b, p: (b, 0, p)),
                  pl.BlockSpec((tq, S), lambda i, b, p: (i, 0))],
        out_specs=pl.BlockSpec((1, tq, LANES), lambda i, b, p: (b, i, p)),
        out_shape=jax.ShapeDtypeStruct((B, S, W), BF16),
        compiler_params=_params("parallel", "parallel", "parallel"),
        name="dilated_attention",
    )(q3, k3, v3, bias)
    return out.reshape(B * S, W)


def _log_sigmoid(x):
    return jnp.minimum(x, 0.0) - jnp.log1p(jnp.exp(-jnp.abs(x)))


def _split_bf16(x):
    hi = x.astype(BF16)
    return hi, (x - hi.astype(F32)).astype(BF16)


def _gla_kernel(q_ref, k_ref, v_ref, r_ref, z_ref, wgf_ref, wgb_ref, bgf_ref, bgb_ref, gn_ref,
                ltri_ref, utri_ref, o_ref, gf_scr, gb_scr, sb_scr):
    S = q_ref.shape[1]
    C = B_CHUNK
    n_chunks = S // C
    kw, vw = q_ref.shape[2], v_ref.shape[2]
    ct = ltri_ref.shape[0]

    def gates(t, carry):
        rows = pl.ds(pl.multiple_of(t * ct, ct), ct)
        zt = z_ref[0, rows, :].astype(BF16)
        lgf = _log_sigmoid(_dot(zt, wgf_ref[0]) + bgf_ref[0]) * (1.0 / B_TAU)
        lgb = _log_sigmoid(_dot(zt, wgb_ref[0]) + bgb_ref[0]) * (1.0 / B_TAU)
        hf, lf = _split_bf16(lgf)
        hb, lb = _split_bf16(lgb)
        gf_scr[rows, :] = _dot(ltri_ref[...], hf) + _dot(ltri_ref[...], lf)
        gb_scr[rows, :] = _dot(utri_ref[...], hb) + _dot(utri_ref[...], lb)
        return carry

    lax.fori_loop(0, S // ct, gates, 0)

    row_v = lax.broadcasted_iota(jnp.int32, (vw, kw), 0)
    lane_k = lax.broadcasted_iota(jnp.int32, (vw, kw), 1)
    state_mask = (row_v // B_DV) == (lane_k // B_DK)
    lane1 = lax.broadcasted_iota(jnp.int32, (1, kw), 1)
    head_lane = [(lane1 // B_DK) == hh for hh in range(kw // B_DK)]
    col_v = lax.broadcasted_iota(jnp.int32, (1, vw), 1)
    head_col = [(col_v // B_DV) == hh for hh in range(vw // B_DV)]
    qi = lax.broadcasted_iota(jnp.int32, (C, kw), 0)
    kj = lax.broadcasted_iota(jnp.int32, (C, kw), 1) % C
    causal = kj <= qi

    def chunk_rows(n):
        return pl.ds(pl.multiple_of(n * C, C), C)

    def state_step(state, k, v, g, gtot):
        kdec = (k * jnp.exp(gtot - g)).astype(BF16)
        upd = _dot_tn(v, kdec)
        return jnp.exp(gtot) * state + jnp.where(state_mask, upd, 0.0)

    def back(i, state):
        n = n_chunks - 1 - i
        rows = chunk_rows(n)
        sb_scr[n] = state.astype(BF16)
        gb = gb_scr[rows, :]
        return state_step(state, k_ref[0, rows, :], v_ref[0, rows, :], gb, gb[0:1, :])

    lax.fori_loop(0, n_chunks, back, jnp.zeros((vw, kw), F32))

    def fwd(n, state):
        rows = chunk_rows(n)
        gf, gb = gf_scr[rows, :], gb_scr[rows, :]
        q, k, v = q_ref[0, rows, :], k_ref[0, rows, :], v_ref[0, rows, :]
        qf, qb = (q * jnp.exp(gf)).astype(BF16), (q * jnp.exp(gb)).astype(BF16)
        kf, kb = (k * jnp.exp(-gf)).astype(BF16), (k * jnp.exp(-gb)).astype(BF16)

        def per_head_rows(t):
            return jnp.concatenate([jnp.where(m, t, jnp.zeros_like(t)) for m in head_lane], axis=0)

        att = jnp.where(causal, _dot_nt(qf, per_head_rows(kf)), _dot_nt(qb, per_head_rows(kb)))
        v_heads = jnp.concatenate([jnp.where(m, v, jnp.zeros_like(v)) for m in head_col], axis=0)
        states = jnp.concatenate([state.astype(BF16), sb_scr[n]], axis=1)
        o = _dot_nt(jnp.concatenate([qf, qb], axis=1), states) + _dot(att.astype(BF16), v_heads)
        normed = []
        for hh in range(vw // B_DV):
            oh = o[:, hh * B_DV:(hh + 1) * B_DV]
            normed.append(oh * lax.rsqrt(jnp.mean(oh * oh, axis=-1, keepdims=True) + EPS))
        gate = _silu(r_ref[0, rows, :].astype(F32))
        o_ref[0, rows, :] = (jnp.concatenate(normed, axis=1) * gn_ref[0] * gate).astype(o_ref.dtype)
        return state_step(state, k, v, gf, gf[C - 1:C, :])

    lax.fori_loop(0, n_chunks, fwd, jnp.zeros((vw, kw), F32))


def _chunk_tri(ct, chunk, upper):
    i = np.arange(ct)[:, None]
    j = np.arange(ct)[None, :]
    same = (i // chunk) == (j // chunk)
    return jnp.asarray(same & ((j >= i) if upper else (j <= i)), BF16)


def gla(qb, kb, vb, rb, z, wg2, bg, g_norm, B, S):
    npair = B_HEADS // 2
    kw, vw = 2 * B_DK, 2 * B_DV
    q3, k3 = qb.reshape(B, S, npair * kw), kb.reshape(B, S, npair * kw)
    v3, r3 = vb.reshape(B, S, npair * vw), rb.reshape(B, S, npair * vw)
    z3 = z.reshape(B, S, 2 * B_RANK)
    zero = jnp.zeros((B_RANK, B_HEADS * B_DK), F32)
    wgf = jnp.concatenate([wg2[0], zero], axis=0).astype(BF16)
    wgb = jnp.concatenate([zero, wg2[1]], axis=0).astype(BF16)
    pairs = lambda w: w.reshape(w.shape[0], npair, kw).transpose(1, 0, 2)
    ct = 256
    pair_spec = lambda shape: pl.BlockSpec((1,) + shape, lambda b, p: (p, 0, 0))
    seq_spec = lambda w: pl.BlockSpec((1, S, w), lambda b, p: (b, 0, p))
    out = pl.pallas_call(
        _gla_kernel,
        grid=(B, npair),
        in_specs=[seq_spec(kw), seq_spec(kw), seq_spec(vw), seq_spec(vw),
                  pl.BlockSpec((1, S, 2 * B_RANK), lambda b, p: (b, 0, 0)),
                  pair_spec((2 * B_RANK, kw)), pair_spec((2 * B_RANK, kw)),
                  pair_spec((1, kw)), pair_spec((1, kw)), pair_spec((1, vw)),
                  _const_spec((ct, ct)), _const_spec((ct, ct))],
        out_specs=seq_spec(vw),
        out_shape=jax.ShapeDtypeStruct((B, S, npair * vw), BF16),
        scratch_shapes=[pltpu.VMEM((S, kw), F32), pltpu.VMEM((S, kw), F32),
                        pltpu.VMEM((S // B_CHUNK, vw, kw), BF16)],
        compiler_params=_params("parallel", "parallel"),
        name="gla",
    )(q3, k3, v3, r3, z3, pairs(wgf), pairs(wgb), pairs(bg[0][None, :]), pairs(bg[1][None, :]),
      g_norm.reshape(npair, 1, vw), _chunk_tri(ct, B_CHUNK, False), _chunk_tri(ct, B_CHUNK, True))
    return out.reshape(B * S, npair * vw)


def _out_proj_kernel(a_ref, b_ref, wa_ref, wb_ref, h_ref, o_ref):
    o_ref[...] = h_ref[...] + _dot(a_ref[...], wa_ref[...]) + _dot(b_ref[...], wb_ref[...])


def out_proj(a, b, w_out, h):
    T, D = h.shape
    wa, wb = w_out[:a.shape[1]].astype(BF16), w_out[a.shape[1]:].astype(BF16)
    tm = ROW_TILE
    return pl.pallas_call(
        _out_proj_kernel,
        grid=(T // tm,),
        in_specs=[_row_spec(tm, a.shape[1]), _row_spec(tm, b.shape[1]), _const_spec(wa.shape),
                  _const_spec(wb.shape), _row_spec(tm, D)],
        out_specs=_row_spec(tm, D),
        out_shape=jax.ShapeDtypeStruct((T, D), F32),
        compiler_params=_params("parallel"),
        name="out_proj",
    )(a, b, wa, wb, h)


def _in_cd_kernel(x_ref, g_ref, wu_ref, wval_ref, wgate_ref, u_ref, gd_ref):
    xn = _rmsnorm(x_ref[...], g_ref[...]).astype(BF16)
    u_ref[...] = _dot(xn, wu_ref[...])
    gd_ref[...] = _dot(xn, wval_ref[...]) * _sigmoid(_dot(xn, wgate_ref[...]))


def in_proj_cd(h, g, w_in):
    T, D = h.shape
    cw = C_GROUP * C_NGROUPS
    dw = (w_in.shape[1] - cw) // 2
    wb = w_in.astype(BF16)
    tm = ROW_TILE
    return pl.pallas_call(
        _in_cd_kernel,
        grid=(T // tm,),
        in_specs=[_row_spec(tm, D), _const_spec((1, D)), _const_spec((D, cw)), _const_spec((D, dw)),
                  _const_spec((D, dw))],
        out_specs=[_row_spec(tm, cw), _row_spec(tm, dw)],
        out_shape=[jax.ShapeDtypeStruct((T, cw), F32), jax.ShapeDtypeStruct((T, dw), F32)],
        compiler_params=_params("parallel"),
        name="in_proj_cd",
    )(h, g[None, :], wb[:, :cw], wb[:, cw:cw + dw], wb[:, cw + dw:])


def _s5_operators(lam_re, lam_im, log_dt, b_re, b_im, c_re, c_im):
    hp = lax.Precision.HIGHEST
    L = S5_CHUNK
    tau = jnp.arange(L + 1, dtype=F32)
    ks, vs, ws, aL = [], [], [], []
    for d in range(2):
        lr, li = lam_re[d], lam_im[d]
        dt = jnp.exp(log_dt[d])[:, None]
        mag = jnp.exp(lr * dt)
        ab_re, ab_im = mag * jnp.cos(li * dt), mag * jnp.sin(li * dt)
        den = lr * lr + li * li
        nr = ab_re - 1.0
        f_re = (nr * lr + ab_im * li) / den
        f_im = (ab_im * lr - nr * li) / den
        bb_re = f_re[..., None] * b_re[d] - f_im[..., None] * b_im[d]
        bb_im = f_re[..., None] * b_im[d] + f_im[..., None] * b_re[d]
        pmag = jnp.exp(lr[None] * dt[None] * tau[:, None, None])
        pr = pmag * jnp.cos(li[None] * dt[None] * tau[:, None, None])
        pi = pmag * jnp.sin(li[None] * dt[None] * tau[:, None, None])
        ca_re = c_re[d][None] * pr[:, :, None, :] - c_im[d][None] * pi[:, :, None, :]
        ca_im = c_re[d][None] * pi[:, :, None, :] + c_im[d][None] * pr[:, :, None, :]
        k = (jnp.einsum('tghp,gpk->tghk', ca_re[:L], bb_re, precision=hp)
             - jnp.einsum('tghp,gpk->tghk', ca_im[:L], bb_im, precision=hp))
        ab_pow_re = pr[:, :, :, None] * bb_re[None] - pi[:, :, :, None] * bb_im[None]
        ab_pow_im = pr[:, :, :, None] * bb_im[None] + pi[:, :, :, None] * bb_re[None]
        order_v = jnp.arange(L - 1, -1, -1) if d == 0 else jnp.arange(L)
        order_w = jnp.arange(1, L + 1) if d == 0 else jnp.arange(L, 0, -1)
        v = jnp.concatenate([ab_pow_re[order_v], ab_pow_im[order_v]], axis=2)
        vs.append(v.transpose(1, 0, 3, 2).reshape(v.shape[1], L * C_GROUP, 2 * C_STATE))
        w = jnp.concatenate([ca_re[order_w], -ca_im[order_w]], axis=3)
        ws.append(w.transpose(1, 3, 0, 2).reshape(w.shape[1], 2 * C_STATE, L * C_GROUP))
        ks.append(k)
        aL.append(jnp.stack([pr[L], pi[L]], axis=0))
    lag = jnp.arange(L)[None, :] - jnp.arange(L)[:, None]
    kf = ks[0][jnp.clip(lag, 0, L - 1)]
    kb = ks[1][jnp.clip(-lag, 0, L - 1)]
    lag5 = lag[:, :, None, None, None]
    m = jnp.where(lag5 >= 0, kf, 0.0) + jnp.where(lag5 <= 0, kb, 0.0)
    m = m.transpose(2, 0, 4, 1, 3).reshape(m.shape[2], L * C_GROUP, L * C_GROUP)
    return m, vs[0], vs[1], ws[0], ws[1], jnp.stack(aL, axis=0)


def _pair_blockdiag(t):
    G, r, c = t.shape
    t = t.reshape(G // 2, 2, r, c)
    z = jnp.zeros((G // 2, r, c), t.dtype)
    return jnp.concatenate([jnp.concatenate([t[:, 0], z], axis=2),
                            jnp.concatenate([z, t[:, 1]], axis=2)], axis=1)


def _s5_pair_operators(params):
    m, vf, vb, wf, wb, aL = _s5_operators(*params)
    P = C_STATE
    mm = _pair_blockdiag(m)
    vcols = [_pair_blockdiag(v[:, :, s]) for v in (vf, vb) for s in (slice(0, P), slice(P, 2 * P))]
    vv = jnp.concatenate(vcols, axis=2)
    wrows = [_pair_blockdiag(w[:, s, :]) for w in (wf, wb) for s in (slice(0, P), slice(P, 2 * P))]
    ww = jnp.concatenate(wrows, axis=1)
    G = aL.shape[2]
    aa = aL.reshape(4, G // 2, 2 * P).transpose(1, 0, 2)
    return mm.astype(BF16), vv.astype(BF16), ww.astype(BF16), aa


def _s5_kernel(x_ref, m_ref, v_ref, w_ref, a_ref, y_ref, v_scr, s_scr, *, rows_per_chunk):
    R = rows_per_chunk
    n_chunks = x_ref.shape[0] // R
    lw = a_ref.shape[2]
    x = x_ref[...]
    v_scr[...] = _dot(x, v_ref[0])
    a = a_ref[0]
    afr, afi, abr, abi = a[0:1], a[1:2], a[2:3], a[3:4]

    def step(n, carry):
        fr, fi, br, bi = carry
        rf = pl.ds(pl.multiple_of(n * R, R), R)
        rb = pl.ds(pl.multiple_of((n_chunks - 1 - n) * R, R), R)
        s_scr[rf, 0:2 * lw] = jnp.concatenate([fr, fi], axis=1).astype(BF16)
        s_scr[rb, 2 * lw:4 * lw] = jnp.concatenate([br, bi], axis=1).astype(BF16)
        vf = v_scr[rf, 0:2 * lw]
        vb = v_scr[rb, 2 * lw:4 * lw]
        return (afr * fr - afi * fi + vf[:, :lw], afr * fi + afi * fr + vf[:, lw:],
                abr * br - abi * bi + vb[:, :lw], abr * bi + abi * br + vb[:, lw:])

    zero = jnp.zeros((R, lw), F32)
    lax.fori_loop(0, n_chunks, step, (zero, zero, zero, zero))
    y_ref[...] = _dot(x, m_ref[0]) + _dot(s_scr[...], w_ref[0])


def s5_bidirectional(u, params, B, S):
    G, Hc, L = C_NGROUPS, C_GROUP, S5_CHUNK
    N = S // L
    mm, vv, ww, aa = _s5_pair_operators(params)
    x = u.reshape(B, N, L, G, Hc).transpose(1, 0, 3, 2, 4).reshape(N * B, G * L * Hc).astype(BF16)
    pw = 2 * L * Hc
    sw = vv.shape[2]
    op_spec = lambda shape: pl.BlockSpec((1,) + shape, lambda p: (p, 0, 0))
    y = pl.pallas_call(
        functools.partial(_s5_kernel, rows_per_chunk=B),
        grid=(G // 2,),
        in_specs=[pl.BlockSpec((N * B, pw), lambda p: (0, p)), op_spec((pw, pw)), op_spec((pw, sw)),
                  op_spec((sw, pw)), op_spec((4, sw // 4))],
        out_specs=pl.BlockSpec((N * B, pw), lambda p: (0, p)),
        out_shape=jax.ShapeDtypeStruct((N * B, G * L * Hc), F32),
        scratch_shapes=[pltpu.VMEM((N * B, sw), F32), pltpu.VMEM((N * B, sw), BF16)],
        compiler_params=_params("parallel"),
        name="s5",
    )(x, mm, vv, ww, aa)
    return y.reshape(N, B, G, L, Hc).transpose(1, 0, 3, 2, 4).reshape(B * S, G * Hc)


def _halo_rows(prev_ref, next_ref, tiles_per_seq):
    t = pl.program_id(0) % tiles_per_seq
    prev = jnp.where(t > 0, prev_ref[...], 0.0)
    nxt = jnp.where(t < tiles_per_seq - 1, next_ref[...], 0.0)
    return prev, nxt


def _cd_out_kernel(y_ref, u_ref, gd_ref, gdp_ref, gdn_ref, h_ref, dskip_ref, wglu_ref, bglu_ref,
                   cw_ref, cb_ref, lng_ref, lnb_ref, wc_ref, wd_ref, o_ref, ext_scr, *, tiles_per_seq):
    tm = y_ref.shape[0]
    z = jax.nn.gelu(y_ref[...] + dskip_ref[...] * u_ref[...])
    o_c = z * _sigmoid(_dot(z.astype(BF16), wglu_ref[...]) + bglu_ref[...])
    prev, nxt = _halo_rows(gdp_ref, gdn_ref, tiles_per_seq)
    ext_scr[0:CONV_HALO, :] = prev
    ext_scr[CONV_HALO:CONV_HALO + tm, :] = gd_ref[...]
    ext_scr[CONV_HALO + tm:CONV_HALO + tm + CONV_HALO, :] = nxt
    base = CONV_HALO - (D_KERNEL - 1) // 2
    acc = jnp.zeros(gd_ref.shape, F32) + cb_ref[...]
    for kk in range(D_KERNEL):
        acc = acc + cw_ref[kk:kk + 1, :] * ext_scr[base + kk:base + kk + tm, :]
    mu = jnp.mean(acc, axis=-1, keepdims=True)
    cen = acc - mu
    var = jnp.mean(cen * cen, axis=-1, keepdims=True)
    o_d = _silu(cen * lax.rsqrt(var + EPS) * lng_ref[...] + lnb_ref[...])
    o_ref[...] = (h_ref[...] + _dot(o_c.astype(BF16), wc_ref[...])
                  + _dot(o_d.astype(BF16), wd_ref[...]))


def cd_out(y, u, gd, h, d_skip, w_glu, b_glu, conv_w, conv_b, ln_g, ln_b, w_out, S):
    T, D = h.shape
    cw, dw = u.shape[1], gd.shape[1]
    tm = ROW_TILE
    tiles_per_seq = S // tm
    hb = tm // CONV_HALO
    n_halo_blocks = T // CONV_HALO
    row = lambda v: v[None, :]
    return pl.pallas_call(
        functools.partial(_cd_out_kernel, tiles_per_seq=tiles_per_seq),
        grid=(T // tm,),
        in_specs=[_row_spec(tm, cw), _row_spec(tm, cw), _row_spec(tm, dw),
                  pl.BlockSpec((CONV_HALO, dw), lambda i: (jnp.maximum(i * hb - 1, 0), 0)),
                  pl.BlockSpec((CONV_HALO, dw), lambda i: (jnp.minimum((i + 1) * hb, n_halo_blocks - 1), 0)),
                  _row_spec(tm, D), _const_spec((1, cw)), _const_spec((cw, cw)), _const_spec((1, cw)),
                  _const_spec((D_KERNEL, dw)), _const_spec((1, dw)), _const_spec((1, dw)),
                  _const_spec((1, dw)), _const_spec((cw, D)), _const_spec((dw, D))],
        out_specs=_row_spec(tm, D),
        out_shape=jax.ShapeDtypeStruct((T, D), F32),
        scratch_shapes=[pltpu.VMEM((tm + 2 * CONV_HALO, dw), F32)],
        compiler_params=_params("parallel"),
        name="cd_out",
    )(y, u, gd, gd, gd, h, row(d_skip), w_glu.astype(BF16), row(b_glu), conv_w, row(conv_b),
      row(ln_g), row(ln_b), w_out[:cw].astype(BF16), w_out[cw:].astype(BF16))


def _kv_kernel(m_ref, g_ref, w_ref, o_ref):
    o_ref[...] = _dot(_rmsnorm(m_ref[...], g_ref[...]).astype(BF16), w_ref[...]).astype(o_ref.dtype)


def mem_kv(mem2, g, wkv):
    T, D = mem2.shape
    tm = ROW_TILE
    return pl.pallas_call(
        _kv_kernel,
        grid=(T // tm,),
        in_specs=[_row_spec(tm, D), _const_spec((1, D)), _const_spec(wkv.shape)],
        out_specs=_row_spec(tm, wkv.shape[1]),
        out_shape=jax.ShapeDtypeStruct((T, wkv.shape[1]), BF16),
        compiler_params=_params("parallel"),
        name="mem_kv",
    )(mem2, g[None, :], wkv.astype(BF16))


def _xattn_kernel(x_ref, g_ref, wq_ref, k_ref, v_ref, wo_ref, o_ref):
    x = x_ref[...]
    hd = x.shape[1] // X_HEADS
    q = (_dot(_rmsnorm(x, g_ref[...]).astype(BF16), wq_ref[...]) * (hd ** -0.5)).astype(BF16)
    heads = []
    for hh in range(X_HEADS):
        cols = slice(hh * hd, (hh + 1) * hd)
        s = _dot_nt(q[:, cols], k_ref[0, :, cols])
        p = jnp.exp(s - jnp.max(s, axis=-1, keepdims=True))
        den = jnp.sum(p, axis=-1, keepdims=True)
        heads.append((_dot(p.astype(BF16), v_ref[0, :, cols]) / den).astype(BF16))
    o_ref[...] = x + _dot(jnp.concatenate(heads, axis=1), wo_ref[...])


def cross_attention(h, g, wq, kv, wo, B, S):
    T, D = h.shape
    M = kv.shape[0] // B
    kv3 = kv.reshape(B, M, 2 * D)
    tm = ROW_TILE
    tps = S // tm
    return pl.pallas_call(
        _xattn_kernel,
        grid=(T // tm,),
        in_specs=[_row_spec(tm, D), _const_spec((1, D)), _const_spec((D, D)),
                  pl.BlockSpec((1, M, D), lambda i: (i // tps, 0, 0)),
                  pl.BlockSpec((1, M, D), lambda i: (i // tps, 0, 1)),
                  _const_spec((D, D))],
        out_specs=_row_spec(tm, D),
        out_shape=jax.ShapeDtypeStruct((T, D), F32),
        compiler_params=_params("parallel"),
        name="cross_attention",
    )(h, g[None, :], wq.astype(BF16), kv3, kv3, wo.astype(BF16))


def _ffn_kernel(x_ref, xp_ref, xn_ref, g_ref, wv_ref, wg_ref, cwv_ref, cwg_ref, cbv_ref, cbg_ref,
                wd_ref, gfin_ref, o_ref, uv_scr, ug_scr, *, tiles_per_seq, n_split, final_norm):
    tm = x_ref.shape[0]
    x = x_ref[...]
    prev, nxt = _halo_rows(xp_ref, xn_ref, tiles_per_seq)
    xe = _rmsnorm(jnp.concatenate([prev, x, nxt], axis=0), g_ref[...]).astype(BF16)
    fw = wv_ref.shape[1] // n_split
    base = FFN_HALO - (FFN_KERNEL - 1) // 2
    acc = x

    def conv(scr, cw_ref, cb_ref, cols):
        out = cb_ref[:, cols]
        for kk in range(FFN_KERNEL):
            out = out + cw_ref[kk:kk + 1, cols] * scr[base + kk:base + kk + tm, :]
        return out

    for c in range(n_split):
        cols = slice(c * fw, (c + 1) * fw)
        uv_scr[...] = _dot(xe, wv_ref[:, cols])
        ug_scr[...] = _dot(xe, wg_ref[:, cols])
        act = _silu(conv(ug_scr, cwg_ref, cbg_ref, cols)) * conv(uv_scr, cwv_ref, cbv_ref, cols)
        acc = acc + _dot(act.astype(BF16), wd_ref[cols, :])
    if final_norm:
        acc = _rmsnorm(acc, gfin_ref[...])
    o_ref[...] = acc


def conv_ffn(h, g, w_up, w_conv, b_conv, w_down, g_final, S, final_norm):
    T, D = h.shape
    F = w_down.shape[0]
    tm = ROW_TILE
    tiles_per_seq = S // tm
    hb = tm // FFN_HALO
    n_halo_blocks = T // FFN_HALO
    n_split = 2
    fw = F // n_split
    wu = w_up.astype(BF16)
    row = lambda v: v[None, :]
    return pl.pallas_call(
        functools.partial(_ffn_kernel, tiles_per_seq=tiles_per_seq, n_split=n_split,
                          final_norm=final_norm),
        grid=(T // tm,),
        in_specs=[_row_spec(tm, D),
                  pl.BlockSpec((FFN_HALO, D), lambda i: (jnp.maximum(i * hb - 1, 0), 0)),
                  pl.BlockSpec((FFN_HALO, D), lambda i: (jnp.minimum((i + 1) * hb, n_halo_blocks - 1), 0)),
                  _const_spec((1, D)), _const_spec((D, F)), _const_spec((D, F)),
                  _const_spec((FFN_KERNEL, F)), _const_spec((FFN_KERNEL, F)),
                  _const_spec((1, F)), _const_spec((1, F)), _const_spec((F, D)), _const_spec((1, D))],
        out_specs=_row_spec(tm, D),
        out_shape=jax.ShapeDtypeStruct((T, D), F32),
        scratch_shapes=[pltpu.VMEM((tm + 2 * FFN_HALO, fw), F32)] * 2,
        compiler_params=_params("parallel"),
        name="conv_ffn",
    )(h, h, h, row(g), wu[:, :F], wu[:, F:], w_conv[:, :F], w_conv[:, F:], row(b_conv[:F]),
      row(b_conv[F:]), w_down.astype(BF16), row(g_final))


def kernel(x, mem, positions, g_mix, g_xattn, g_mem, w_xq, w_xkv, w_xo, g_ffn, w_up, w_conv_ffn,
           b_conv_ffn, w_down, w_in_ab, w_out_ab, gla_wg2, gla_bg, gla_norm, w_in_cd, w_out_cd,
           s5_lam_re, s5_lam_im, s5_log_dt, s5_b_re, s5_b_im, s5_c_re, s5_c_im, s5_d, s5_w_glu,
           s5_b_glu, conv_w, conv_b, conv_ln_g, conv_ln_b, g_final):
    B, S, D = x.shape
    depth = g_mix.shape[0]
    assert S % ROW_TILE == 0 and S % B_CHUNK == 0 and S % S5_CHUNK == 0
    h = x.reshape(B * S, D)
    mem2 = mem.reshape(-1, D)
    cos, sin = rope_tables(positions)
    for layer in range(depth):
        i = layer // 2
        if layer % 2 == 0:
            qa, ka, va, qb, kb, vb, rb, z = in_proj_ab(h, g_mix[layer], cos, sin, w_in_ab[i])
            o_a = dilated_attention(qa, ka, va, B, S)
            o_b = gla(qb, kb, vb, rb, z, gla_wg2[i], gla_bg[i], gla_norm[i], B, S)
            h = out_proj(o_a, o_b, w_out_ab[i], h)
        else:
            u, gd = in_proj_cd(h, g_mix[layer], w_in_cd[i])
            y = s5_bidirectional(u, (s5_lam_re[i], s5_lam_im[i], s5_log_dt[i], s5_b_re[i],
                                     s5_b_im[i], s5_c_re[i], s5_c_im[i]), B, S)
            h = cd_out(y, u, gd, h, s5_d[i], s5_w_glu[i], s5_b_glu[i], conv_w[i], conv_b[i],
                       conv_ln_g[i], conv_ln_b[i], w_out_cd[i], S)
        kv = mem_kv(mem2, g_mem[layer], w_xkv[layer])
        h = cross_attention(h, g_xattn[layer], w_xq[layer], kv, w_xo[layer], B, S)
        h = conv_ffn(h, g_ffn[layer], w_up[layer], w_conv_ffn[layer], b_conv_ffn[layer],
                     w_down[layer], g_final, S, final_norm=(layer == depth - 1))
    return h.reshape(B, S, D)
```

```python
import functools
import math

import jax
import jax.numpy as jnp
import numpy as np
from jax import lax
from jax.experimental import pallas as pl
from jax.experimental.pallas import tpu as pltpu

F32 = jnp.float32
BF16 = jnp.bfloat16
EPS = 1e-6

LANES = 128
SUBLANES = 8
VMEM_LIMIT_BYTES = 52 * 1024 * 1024

A_HEADS, A_HEAD_DIM = 8, 64
A_PATTERNS = ((128, 1), (512, 4), (2048, 16))
ROPE_THETA = 10000.0
B_HEADS, B_DK, B_DV = 4, 64, 128
B_RANK, B_TAU, B_CHUNK = 16, 16.0, 64
C_GROUP, C_NGROUPS, C_STATE = 16, 32, 64
S5_CHUNK = 16
D_KERNEL = 31
X_HEADS = 4
FFN_KERNEL = 3
NEG_BIG = -1e30

ROW_TILE = 512
CONV_HALO = 16
FFN_HALO = 8


def _params(*sem):
    return pltpu.CompilerParams(dimension_semantics=sem, vmem_limit_bytes=VMEM_LIMIT_BYTES)


def _const_spec(shape):
    zeros = (0,) * len(shape)
    return pl.BlockSpec(shape, lambda *_: zeros, pipeline_mode=pl.Buffered(1))


def _row_spec(tm, width):
    return pl.BlockSpec((tm, width), lambda i: (i, 0))


def _dot(a, b):
    return jnp.dot(a, b, preferred_element_type=F32)


def _dot_nt(a, b):
    return lax.dot_general(a, b, (((1,), (1,)), ((), ())), preferred_element_type=F32)


def _dot_tn(a, b):
    return lax.dot_general(a, b, (((0,), (0,)), ((), ())), preferred_element_type=F32)


def _rmsnorm(x, g):
    return x * lax.rsqrt(jnp.mean(x * x, axis=-1, keepdims=True) + EPS) * g


def _sigmoid(x):
    return 1.0 / (1.0 + jnp.exp(-x))


def _silu(x):
    return x * _sigmoid(x)


def _rope_table_kernel(pos_ref, invf_ref, cos_ref, sin_ref):
    ang = pos_ref[...].astype(F32) * invf_ref[...]
    lane = lax.broadcasted_iota(jnp.int32, (1, LANES), 1)
    sign = jnp.where((lane % A_HEAD_DIM) < A_HEAD_DIM // 2, -1.0, 1.0)
    cos_ref[...] = jnp.cos(ang)
    sin_ref[...] = jnp.sin(ang) * sign


def rope_tables(positions):
    T = positions.size
    tm = 1024
    inv_freq = ROPE_THETA ** (-jnp.arange(0, A_HEAD_DIM, 2, dtype=F32) / A_HEAD_DIM)
    invf = jnp.tile(inv_freq, LANES // (A_HEAD_DIM // 2))[None, :]
    return pl.pallas_call(
        _rope_table_kernel,
        grid=(T // tm,),
        in_specs=[_row_spec(tm, 1), _const_spec((1, LANES))],
        out_specs=[_row_spec(tm, LANES), _row_spec(tm, LANES)],
        out_shape=[jax.ShapeDtypeStruct((T, LANES), F32)] * 2,
        compiler_params=_params("parallel"),
        name="rope_tables",
    )(positions.reshape(T, 1), invf)


def _in_ab_kernel(x_ref, g_ref, cos_ref, sin_ref, wqk_ref, wva_ref, wqkb_ref, wvr_ref, wz_ref,
                  qa_ref, ka_ref, va_ref, qb_ref, kb_ref, vb_ref, rb_ref, z_ref):
    xn = _rmsnorm(x_ref[...], g_ref[...]).astype(BF16)
    qk = _dot(xn, wqk_ref[...])
    width = qk.shape[1]
    reps = width // LANES
    cos = jnp.concatenate([cos_ref[...]] * reps, axis=1)
    sin = jnp.concatenate([sin_ref[...]] * reps, axis=1)
    half = A_HEAD_DIM // 2
    lane = lax.broadcasted_iota(jnp.int32, (1, width), 1)
    first_half = (lane % A_HEAD_DIM) < half
    partner = jnp.where(first_half, pltpu.roll(qk, width - half, axis=1), pltpu.roll(qk, half, axis=1))
    roped = qk * cos + partner * sin
    aw = width // 2
    qa_ref[...] = (roped[:, :aw] * (A_HEAD_DIM ** -0.5)).astype(BF16)
    ka_ref[...] = roped[:, aw:].astype(BF16)
    va_ref[...] = _dot(xn, wva_ref[...]).astype(BF16)
    qkb = _dot(xn, wqkb_ref[...])
    kw = qkb.shape[1] // 2
    qb_ref[...] = qkb[:, :kw] * (B_DK ** -0.5)
    kb_ref[...] = qkb[:, kw:]
    vr = _dot(xn, wvr_ref[...])
    vw = vr.shape[1] // 2
    vb_ref[...] = vr[:, :vw].astype(BF16)
    rb_ref[...] = vr[:, vw:].astype(BF16)
    z_ref[...] = _dot(xn, wz_ref[...])


def in_proj_ab(h, g, cos, sin, w_in):
    T, D = h.shape
    aw, kw, vw = A_HEADS * A_HEAD_DIM, B_HEADS * B_DK, B_HEADS * B_DV
    o = np.cumsum([0, aw, aw, aw, kw, kw, vw, vw, 2 * B_RANK])
    wb = w_in.astype(BF16)
    wqk, wva, wqkb, wvr, wz = (wb[:, o[0]:o[2]], wb[:, o[2]:o[3]], wb[:, o[3]:o[5]],
                               wb[:, o[5]:o[7]], wb[:, o[7]:o[8]])
    tm = ROW_TILE
    outs = [(aw, BF16), (aw, BF16), (aw, BF16), (kw, F32), (kw, F32), (vw, BF16), (vw, BF16),
            (2 * B_RANK, F32)]
    return pl.pallas_call(
        _in_ab_kernel,
        grid=(T // tm,),
        in_specs=[_row_spec(tm, D), _const_spec((1, D)), _row_spec(tm, LANES), _row_spec(tm, LANES),
                  _const_spec(wqk.shape), _const_spec(wva.shape), _const_spec(wqkb.shape),
                  _const_spec(wvr.shape), _const_spec(wz.shape)],
        out_specs=[_row_spec(tm, w) for w, _ in outs],
        out_shape=[jax.ShapeDtypeStruct((T, w), dt) for w, dt in outs],
        compiler_params=_params("parallel"),
        name="in_proj_ab",
    )(h, g[None, :], cos, sin, wqk, wva, wqkb, wvr, wz)


def _attn_kernel(q_ref, k_ref, v_ref, bias_ref, o_ref):
    q, k, v = q_ref[0], k_ref[0], v_ref[0]
    bias = bias_ref[...]
    lane = lax.broadcasted_iota(jnp.int32, (1, LANES), 1)
    outs = []
    for hh in range(LANES // A_HEAD_DIM):
        qh = jnp.where((lane // A_HEAD_DIM) == hh, q, jnp.zeros_like(q))
        s = _dot_nt(qh, k) + bias
        p = jnp.exp(s - jnp.max(s, axis=-1, keepdims=True))
        den = jnp.sum(p, axis=-1, keepdims=True)
        outs.append(_dot(p.astype(BF16), v) / den)
    o_ref[0] = jnp.where(lane < A_HEAD_DIM, outs[0], outs[1]).astype(o_ref.dtype)


def _mixture_bias_kernel(o_ref):
    tq, S = o_ref.shape
    row = lax.broadcasted_iota(jnp.int32, (tq, S), 0) + pl.program_id(0) * tq
    d = lax.broadcasted_iota(jnp.int32, (tq, S), 1) - row
    dist = jnp.abs(d)
    count = jnp.zeros((tq, S), F32)
    for window, dil in A_PATTERNS:
        hit = jnp.where((d & (dil - 1)) == 0, dist, S * 2) <= window // 2
        count = count + jnp.where(hit, 1.0, 0.0)
    o_ref[...] = jnp.where(count > 0.5, jnp.log(jnp.maximum(count, 1.0)), NEG_BIG)


def _mixture_bias(S):
    tq = 256
    return pl.pallas_call(
        _mixture_bias_kernel,
        grid=(S // tq,),
        out_specs=pl.BlockSpec((tq, S), lambda i: (i, 0)),
        out_shape=jax.ShapeDtypeStruct((S, S), F32),
        compiler_params=_params("parallel"),
        name="mixture_bias",
    )()


def dilated_attention(qa, ka, va, B, S):
    W = qa.shape[-1]
    tq = 256
    q3, k3, v3 = (t.reshape(B, S, W) for t in (qa, ka, va))
    bias = _mixture_bias(S)
    out = pl.pallas_call(
        _attn_kernel,
        grid=(S // tq, B, W // LANES),
        in_specs=[pl.BlockSpec((1, tq, LANES), lambda i, b, p: (b, i, p)),
                  pl.BlockSpec((1, S, LANES), lambda i, b, p: (b, 0, p)),
                  pl.BlockSpec((1, S, LANES), lambda i, b, p: (b, 0, p)),
                  pl.BlockSpec((tq, S), lambda i, b, p: (i, 0))],
        out_specs=pl.BlockSpec((1, tq, LANES), lambda i, b, p: (b, i, p)),
        out_shape=jax.ShapeDtypeStruct((B, S, W), BF16),
        compiler_params=_params("parallel", "parallel", "parallel"),
        name="dilated_attention",
    )(q3, k3, v3, bias)
    return out.reshape(B * S, W)


def _log_sigmoid(x):
    return jnp.minimum(x, 0.0) - jnp.log1p(jnp.exp(-jnp.abs(x)))


def _split_bf16(x):
    hi = x.astype(BF16)
    return hi, (x - hi.astype(F32)).astype(BF16)


def _gla_kernel(q_ref, k_ref, v_ref, r_ref, z_ref, wgf_ref, wgb_ref, bgf_ref, bgb_ref, gn_ref,
                ltri_ref, utri_ref, o_ref, gf_scr, gb_scr, sb_scr):
    S = q_ref.shape[1]
    C = B_CHUNK
    n_chunks = S // C
    kw, vw = q_ref.shape[2], v_ref.shape[2]
    ct = ltri_ref.shape[0]

    def gates(t, carry):
        rows = pl.ds(pl.multiple_of(t * ct, ct), ct)
        zt = z_ref[0, rows, :].astype(BF16)
        lgf = _log_sigmoid(_dot(zt, wgf_ref[0]) + bgf_ref[0]) * (1.0 / B_TAU)
        lgb = _log_sigmoid(_dot(zt, wgb_ref[0]) + bgb_ref[0]) * (1.0 / B_TAU)
        hf, lf = _split_bf16(lgf)
        hb, lb = _split_bf16(lgb)
        gf_scr[rows, :] = _dot(ltri_ref[...], hf) + _dot(ltri_ref[...], lf)
        gb_scr[rows, :] = _dot(utri_ref[...], hb) + _dot(utri_ref[...], lb)
        return carry

    lax.fori_loop(0, S // ct, gates, 0)

    row_v = lax.broadcasted_iota(jnp.int32, (vw, kw), 0)
    lane_k = lax.broadcasted_iota(jnp.int32, (vw, kw), 1)
    state_mask = (row_v // B_DV) == (lane_k // B_DK)
    lane1 = lax.broadcasted_iota(jnp.int32, (1, kw), 1)
    head_lane = [(lane1 // B_DK) == hh for hh in range(kw // B_DK)]
    col_v = lax.broadcasted_iota(jnp.int32, (1, vw), 1)
    head_col = [(col_v // B_DV) == hh for hh in range(vw // B_DV)]
    qi = lax.broadcasted_iota(jnp.int32, (C, kw), 0)
    kj = lax.broadcasted_iota(jnp.int32, (C, kw), 1) % C
    causal = kj <= qi

    def chunk_rows(n):
        return pl.ds(pl.multiple_of(n * C, C), C)

    def state_step(state, k, v, g, gtot):
        kdec = (k * jnp.exp(gtot - g)).astype(BF16)
        upd = _dot_tn(v, kdec)
        return jnp.exp(gtot) * state + jnp.where(state_mask, upd, 0.0)

    def back(i, state):
        n = n_chunks - 1 - i
        rows = chunk_rows(n)
        sb_scr[n] = state.astype(BF16)
        gb = gb_scr[rows, :]
        return state_step(state, k_ref[0, rows, :], v_ref[0, rows, :], gb, gb[0:1, :])

    lax.fori_loop(0, n_chunks, back, jnp.zeros((vw, kw), F32))

    def fwd(n, state):
        rows = chunk_rows(n)
        gf, gb = gf_scr[rows, :], gb_scr[rows, :]
        q, k, v = q_ref[0, rows, :], k_ref[0, rows, :], v_ref[0, rows, :]
        qf, qb = (q * jnp.exp(gf)).astype(BF16), (q * jnp.exp(gb)).astype(BF16)
        kf, kb = (k * jnp.exp(-gf)).astype(BF16), (k * jnp.exp(-gb)).astype(BF16)

        def per_head_rows(t):
            return jnp.concatenate([jnp.where(m, t, jnp.zeros_like(t)) for m in head_lane], axis=0)

        att = jnp.where(causal, _dot_nt(qf, per_head_rows(kf)), _dot_nt(qb, per_head_rows(kb)))
        v_heads = jnp.concatenate([jnp.where(m, v, jnp.zeros_like(v)) for m in head_col], axis=0)
        states = jnp.concatenate([state.astype(BF16), sb_scr[n]], axis=1)
        o = _dot_nt(jnp.concatenate([qf, qb], axis=1), states) + _dot(att.astype(BF16), v_heads)
        normed = []
        for hh in range(vw // B_DV):
            oh = o[:, hh * B_DV:(hh + 1) * B_DV]
            normed.append(oh * lax.rsqrt(jnp.mean(oh * oh, axis=-1, keepdims=True) + EPS))
        gate = _silu(r_ref[0, rows, :].astype(F32))
        o_ref[0, rows, :] = (jnp.concatenate(normed, axis=1) * gn_ref[0] * gate).astype(o_ref.dtype)
        return state_step(state, k, v, gf, gf[C - 1:C, :])

    lax.fori_loop(0, n_chunks, fwd, jnp.zeros((vw, kw), F32))


def _chunk_tri(ct, chunk, upper):
    i = np.arange(ct)[:, None]
    j = np.arange(ct)[None, :]
    same = (i // chunk) == (j // chunk)
    return jnp.asarray(same & ((j >= i) if upper else (j <= i)), BF16)


def gla(qb, kb, vb, rb, z, wg2, bg, g_norm, B, S):
    npair = B_HEADS // 2
    kw, vw = 2 * B_DK, 2 * B_DV
    q3, k3 = qb.reshape(B, S, npair * kw), kb.reshape(B, S, npair * kw)
    v3, r3 = vb.reshape(B, S, npair * vw), rb.reshape(B, S, npair * vw)
    z3 = z.reshape(B, S, 2 * B_RANK)
    zero = jnp.zeros((B_RANK, B_HEADS * B_DK), F32)
    wgf = jnp.concatenate([wg2[0], zero], axis=0).astype(BF16)
    wgb = jnp.concatenate([zero, wg2[1]], axis=0).astype(BF16)
    pairs = lambda w: w.reshape(w.shape[0], npair, kw).transpose(1, 0, 2)
    ct = 256
    pair_spec = lambda shape: pl.BlockSpec((1,) + shape, lambda b, p: (p, 0, 0))
    seq_spec = lambda w: pl.BlockSpec((1, S, w), lambda b, p: (b, 0, p))
    out = pl.pallas_call(
        _gla_kernel,
        grid=(B, npair),
        in_specs=[seq_spec(kw), seq_spec(kw), seq_spec(vw), seq_spec(vw),
                  pl.BlockSpec((1, S, 2 * B_RANK), lambda b, p: (b, 0, 0)),
                  pair_spec((2 * B_RANK, kw)), pair_spec((2 * B_RANK, kw)),
                  pair_spec((1, kw)), pair_spec((1, kw)), pair_spec((1, vw)),
                  _const_spec((ct, ct)), _const_spec((ct, ct))],
        out_specs=seq_spec(vw),
        out_shape=jax.ShapeDtypeStruct((B, S, npair * vw), BF16),
        scratch_shapes=[pltpu.VMEM((S, kw), F32), pltpu.VMEM((S, kw), F32),
                        pltpu.VMEM((S // B_CHUNK, vw, kw), BF16)],
        compiler_params=_params("parallel", "parallel"),
        name="gla",
    )(q3, k3, v3, r3, z3, pairs(wgf), pairs(wgb), pairs(bg[0][None, :]), pairs(bg[1][None, :]),
      g_norm.reshape(npair, 1, vw), _chunk_tri(ct, B_CHUNK, False), _chunk_tri(ct, B_CHUNK, True))
    return out.reshape(B * S, npair * vw)


def _out_proj_kernel(a_ref, b_ref, wa_ref, wb_ref, h_ref, o_ref):
    o_ref[...] = h_ref[...] + _dot(a_ref[...], wa_ref[...]) + _dot(b_ref[...], wb_ref[...])


def out_proj(a, b, w_out, h):
    T, D = h.shape
    wa, wb = w_out[:a.shape[1]].astype(BF16), w_out[a.shape[1]:].astype(BF16)
    tm = ROW_TILE
    return pl.pallas_call(
        _out_proj_kernel,
        grid=(T // tm,),
        in_specs=[_row_spec(tm, a.shape[1]), _row_spec(tm, b.shape[1]), _const_spec(wa.shape),
                  _const_spec(wb.shape), _row_spec(tm, D)],
        out_specs=_row_spec(tm, D),
        out_shape=jax.ShapeDtypeStruct((T, D), F32),
        compiler_params=_params("parallel"),
        name="out_proj",
    )(a, b, wa, wb, h)


def _atom_transpose(x):
    rows, width = x.shape
    r = lax.broadcasted_iota(jnp.int32, (rows, width), 0)
    a = lax.broadcasted_iota(jnp.int32, (rows, width), 1) // C_GROUP
    for s in range(3):
        d = 1 << s
        rbit = (r & d) != 0
        abit = (a & d) != 0
        partner_row = jnp.where(rbit, pltpu.roll(x, d, axis=0), pltpu.roll(x, rows - d, axis=0))
        moved = jnp.where(abit, pltpu.roll(partner_row, C_GROUP * d, axis=1),
                          pltpu.roll(partner_row, width - C_GROUP * d, axis=1))
        x = jnp.where(rbit == abit, x, moved)
    return x


def _pack_groups(u, x_ref, z_scr):
    tm, width = u.shape
    z = _atom_transpose(u)
    nch = tm // S5_CHUNK
    for k in range(width // LANES):
        z_scr[k] = z[:, LANES * k:LANES * (k + 1)]
    for k in range(width // LANES):
        for g8 in range(SUBLANES):
            lo = z_scr[k, pl.ds(g8, nch, stride=S5_CHUNK), :]
            hi = z_scr[k, pl.ds(SUBLANES + g8, nch, stride=S5_CHUNK), :]
            x_ref[SUBLANES * k + g8, 0, :, :] = jnp.concatenate([lo, hi], axis=1).astype(x_ref.dtype)


def _unpack_groups(y_ref, z_scr):
    n_tiles, tm, _ = z_scr.shape
    nch = tm // S5_CHUNK
    for k in range(n_tiles):
        for g8 in range(SUBLANES):
            y = y_ref[SUBLANES * k + g8, 0, :, :]
            z_scr[k, pl.ds(g8, nch, stride=S5_CHUNK), :] = y[:, :LANES]
            z_scr[k, pl.ds(SUBLANES + g8, nch, stride=S5_CHUNK), :] = y[:, LANES:]
    return _atom_transpose(jnp.concatenate([z_scr[k] for k in range(n_tiles)], axis=1))


def _in_cd_kernel(x_ref, g_ref, wu_ref, wval_ref, wgate_ref, u_ref, gd_ref, xg_ref, z_scr):
    xn = _rmsnorm(x_ref[...], g_ref[...]).astype(BF16)
    u = _dot(xn, wu_ref[...])
    u_ref[...] = u
    _pack_groups(u, xg_ref, z_scr)
    gd_ref[...] = _dot(xn, wval_ref[...]) * _sigmoid(_dot(xn, wgate_ref[...]))


def in_proj_cd(h, g, w_in, B, S):
    T, D = h.shape
    cw = C_GROUP * C_NGROUPS
    dw = (w_in.shape[1] - cw) // 2
    wb = w_in.astype(BF16)
    tm = ROW_TILE
    tps = S // tm
    xw = S5_CHUNK * C_GROUP
    return pl.pallas_call(
        _in_cd_kernel,
        grid=(T // tm,),
        in_specs=[_row_spec(tm, D), _const_spec((1, D)), _const_spec((D, cw)), _const_spec((D, dw)),
                  _const_spec((D, dw))],
        out_specs=[_row_spec(tm, cw), _row_spec(tm, dw),
                   pl.BlockSpec((C_NGROUPS, 1, tm // S5_CHUNK, xw), lambda i: (0, i // tps, i % tps, 0))],
        out_shape=[jax.ShapeDtypeStruct((T, cw), F32), jax.ShapeDtypeStruct((T, dw), F32),
                   jax.ShapeDtypeStruct((C_NGROUPS, B, S // S5_CHUNK, xw), BF16)],
        scratch_shapes=[pltpu.VMEM((cw // LANES, tm, LANES), F32)],
        compiler_params=_params("parallel"),
        name="in_proj_cd",
    )(h, g[None, :], wb[:, :cw], wb[:, cw:cw + dw], wb[:, cw + dw:])


def _s5_operators(lam_re, lam_im, log_dt, b_re, b_im, c_re, c_im):
    hp = lax.Precision.HIGHEST
    L = S5_CHUNK
    tau = jnp.arange(L + 1, dtype=F32)
    ks, vs, ws, aL = [], [], [], []
    for d in range(2):
        lr, li = lam_re[d], lam_im[d]
        dt = jnp.exp(log_dt[d])[:, None]
        mag = jnp.exp(lr * dt)
        ab_re, ab_im = mag * jnp.cos(li * dt), mag * jnp.sin(li * dt)
        den = lr * lr + li * li
        nr = ab_re - 1.0
        f_re = (nr * lr + ab_im * li) / den
        f_im = (ab_im * lr - nr * li) / den
        bb_re = f_re[..., None] * b_re[d] - f_im[..., None] * b_im[d]
        bb_im = f_re[..., None] * b_im[d] + f_im[..., None] * b_re[d]
        pmag = jnp.exp(lr[None] * dt[None] * tau[:, None, None])
        pr = pmag * jnp.cos(li[None] * dt[None] * tau[:, None, None])
        pi = pmag * jnp.sin(li[None] * dt[None] * tau[:, None, None])
        ca_re = c_re[d][None] * pr[:, :, None, :] - c_im[d][None] * pi[:, :, None, :]
        ca_im = c_re[d][None] * pi[:, :, None, :] + c_im[d][None] * pr[:, :, None, :]
        k = (jnp.einsum('tghp,gpk->tghk', ca_re[:L], bb_re, precision=hp)
             - jnp.einsum('tghp,gpk->tghk', ca_im[:L], bb_im, precision=hp))
        ab_pow_re = pr[:, :, :, None] * bb_re[None] - pi[:, :, :, None] * bb_im[None]
        ab_pow_im = pr[:, :, :, None] * bb_im[None] + pi[:, :, :, None] * bb_re[None]
        order_v = (lambda t: t[L - 1::-1]) if d == 0 else (lambda t: t[:L])
        order_w = (lambda t: t[1:L + 1]) if d == 0 else (lambda t: t[L:0:-1])
        v = jnp.concatenate([order_v(ab_pow_re), order_v(ab_pow_im)], axis=2)
        vs.append(v.transpose(1, 0, 3, 2).reshape(v.shape[1], L * C_GROUP, 2 * C_STATE))
        w = jnp.concatenate([order_w(ca_re), -order_w(ca_im)], axis=3)
        ws.append(w.transpose(1, 3, 0, 2).reshape(w.shape[1], 2 * C_STATE, L * C_GROUP))
        ks.append(k)
        aL.append(jnp.stack([pr[L], pi[L]], axis=0))
    by_lag = jnp.concatenate([ks[1][:0:-1], (ks[0][0] + ks[1][0])[None], ks[0][1:]], axis=0)
    m = jnp.stack([by_lag[L - 1 - j:2 * L - 1 - j] for j in range(L)], axis=0)
    m = m.transpose(2, 0, 4, 1, 3).reshape(m.shape[2], L * C_GROUP, L * C_GROUP)
    return m, vs[0], vs[1], ws[0], ws[1], jnp.stack(aL, axis=0)


def _pair_blockdiag(t):
    G, r, c = t.shape
    t = t.reshape(G // 2, 2, r, c)
    z = jnp.zeros((G // 2, r, c), t.dtype)
    return jnp.concatenate([jnp.concatenate([t[:, 0], z], axis=2),
                            jnp.concatenate([z, t[:, 1]], axis=2)], axis=1)


def _s5_pair_operators(params):
    m, vf, vb, wf, wb, aL = _s5_operators(*params)
    P = C_STATE
    mm = _pair_blockdiag(m)
    vcols = [_pair_blockdiag(v[:, :, s]) for v in (vf, vb) for s in (slice(0, P), slice(P, 2 * P))]
    vv = jnp.concatenate(vcols, axis=2)
    wrows = [_pair_blockdiag(w[:, s, :]) for w in (wf, wb) for s in (slice(0, P), slice(P, 2 * P))]
    ww = jnp.concatenate(wrows, axis=1)
    G = aL.shape[2]
    aa = aL.reshape(4, G // 2, 2 * P).transpose(1, 0, 2)
    return mm.astype(BF16), vv.astype(BF16), ww.astype(BF16), aa


def _s5_kernel(x_ref, m_ref, v_ref, w_ref, a_ref, y_ref, v_scr, s_scr):
    _, B, N, half = x_ref.shape
    lw = a_ref.shape[2]
    x = jnp.concatenate([x_ref[0].reshape(B * N, half), x_ref[1].reshape(B * N, half)], axis=1)
    v = _dot(x, v_ref[0])
    for p in range(4):
        v_scr[p] = v[:, lw * p:lw * (p + 1)]
    a = a_ref[0]
    afr, afi, abr, abi = a[0:1], a[1:2], a[2:3], a[3:4]

    def step(n, carry):
        fr, fi, br, bi = carry
        rf = pl.ds(n, B, stride=N)
        rb = pl.ds(N - 1 - n, B, stride=N)
        s_scr[0, rf, :] = fr
        s_scr[1, rf, :] = fi
        s_scr[2, rb, :] = br
        s_scr[3, rb, :] = bi
        return (afr * fr - afi * fi + v_scr[0, rf, :], afr * fi + afi * fr + v_scr[1, rf, :],
                abr * br - abi * bi + v_scr[2, rb, :], abr * bi + abi * br + v_scr[3, rb, :])

    zero = jnp.zeros((B, lw), F32)
    lax.fori_loop(0, N, step, (zero, zero, zero, zero))
    s = jnp.concatenate([s_scr[p] for p in range(4)], axis=1).astype(BF16)
    y = _dot(x, m_ref[0]) + _dot(s, w_ref[0])
    y_ref[0] = y[:, :half].reshape(B, N, half)
    y_ref[1] = y[:, half:].reshape(B, N, half)


def s5_bidirectional(xg, params):
    G, B, N, half = xg.shape
    mm, vv, ww, aa = _s5_pair_operators(params)
    pw = 2 * half
    sw = vv.shape[2]
    op_spec = lambda shape: pl.BlockSpec((1,) + shape, lambda p: (p, 0, 0))
    pair_spec = pl.BlockSpec((2, B, N, half), lambda p: (p, 0, 0, 0))
    return pl.pallas_call(
        _s5_kernel,
        grid=(G // 2,),
        in_specs=[pair_spec, op_spec((pw, pw)), op_spec((pw, sw)), op_spec((sw, pw)),
                  op_spec((4, sw // 4))],
        out_specs=pair_spec,
        out_shape=jax.ShapeDtypeStruct((G, B, N, half), F32),
        scratch_shapes=[pltpu.VMEM((4, B * N, sw // 4), F32)] * 2,
        compiler_params=_params("parallel"),
        name="s5",
    )(xg, mm, vv, ww, aa)


def _halo_rows(prev_ref, next_ref, tiles_per_seq):
    t = pl.program_id(0) % tiles_per_seq
    prev = jnp.where(t > 0, prev_ref[...], 0.0)
    nxt = jnp.where(t < tiles_per_seq - 1, next_ref[...], 0.0)
    return prev, nxt


def _cd_out_kernel(y_ref, u_ref, gd_ref, gdp_ref, gdn_ref, h_ref, dskip_ref, wglu_ref, bglu_ref,
                   cw_ref, cb_ref, lng_ref, lnb_ref, wc_ref, wd_ref, o_ref, ext_scr, z_scr, *, tiles_per_seq):
    tm = u_ref.shape[0]
    z = jax.nn.gelu(_unpack_groups(y_ref, z_scr) + dskip_ref[...] * u_ref[...])
    o_c = z * _sigmoid(_dot(z.astype(BF16), wglu_ref[...]) + bglu_ref[...])
    prev, nxt = _halo_rows(gdp_ref, gdn_ref, tiles_per_seq)
    ext_scr[0:CONV_HALO, :] = prev
    ext_scr[CONV_HALO:CONV_HALO + tm, :] = gd_ref[...]
    ext_scr[CONV_HALO + tm:CONV_HALO + tm + CONV_HALO, :] = nxt
    base = CONV_HALO - (D_KERNEL - 1) // 2
    acc = jnp.zeros(gd_ref.shape, F32) + cb_ref[...]
    for kk in range(D_KERNEL):
        acc = acc + cw_ref[kk:kk + 1, :] * ext_scr[base + kk:base + kk + tm, :]
    mu = jnp.mean(acc, axis=-1, keepdims=True)
    cen = acc - mu
    var = jnp.mean(cen * cen, axis=-1, keepdims=True)
    o_d = _silu(cen * lax.rsqrt(var + EPS) * lng_ref[...] + lnb_ref[...])
    o_ref[...] = (h_ref[...] + _dot(o_c.astype(BF16), wc_ref[...])
                  + _dot(o_d.astype(BF16), wd_ref[...]))


def cd_out(yg, u, gd, h, d_skip, w_glu, b_glu, conv_w, conv_b, ln_g, ln_b, w_out, S):
    T, D = h.shape
    cw, dw = u.shape[1], gd.shape[1]
    tm = ROW_TILE
    tiles_per_seq = tps = S // tm
    hb = tm // CONV_HALO
    n_halo_blocks = T // CONV_HALO
    row = lambda v: v[None, :]
    return pl.pallas_call(
        functools.partial(_cd_out_kernel, tiles_per_seq=tiles_per_seq),
        grid=(T // tm,),
        in_specs=[pl.BlockSpec((yg.shape[0], 1, tm // S5_CHUNK, yg.shape[3]),
                               lambda i: (0, i // tps, i % tps, 0)),
                  _row_spec(tm, cw), _row_spec(tm, dw),
                  pl.BlockSpec((CONV_HALO, dw), lambda i: (jnp.maximum(i * hb - 1, 0), 0)),
                  pl.BlockSpec((CONV_HALO, dw), lambda i: (jnp.minimum((i + 1) * hb, n_halo_blocks - 1), 0)),
                  _row_spec(tm, D), _const_spec((1, cw)), _const_spec((cw, cw)), _const_spec((1, cw)),
                  _const_spec((D_KERNEL, dw)), _const_spec((1, dw)), _const_spec((1, dw)),
                  _const_spec((1, dw)), _const_spec((cw, D)), _const_spec((dw, D))],
        out_specs=_row_spec(tm, D),
        out_shape=jax.ShapeDtypeStruct((T, D), F32),
        scratch_shapes=[pltpu.VMEM((tm + 2 * CONV_HALO, dw), F32), pltpu.VMEM((cw // LANES, tm, LANES), F32)],
        compiler_params=_params("parallel"),
        name="cd_out",
    )(yg, u, gd, gd, gd, h, row(d_skip), w_glu.astype(BF16), row(b_glu), conv_w, row(conv_b),
      row(ln_g), row(ln_b), w_out[:cw].astype(BF16), w_out[cw:].astype(BF16))


def _kv_kernel(m_ref, g_ref, w_ref, o_ref):
    o_ref[...] = _dot(_rmsnorm(m_ref[...], g_ref[...]).astype(BF16), w_ref[...]).astype(o_ref.dtype)


def mem_kv(mem2, g, wkv):
    T, D = mem2.shape
    tm = ROW_TILE
    return pl.pallas_call(
        _kv_kernel,
        grid=(T // tm,),
        in_specs=[_row_spec(tm, D), _const_spec((1, D)), _const_spec(wkv.shape)],
        out_specs=_row_spec(tm, wkv.shape[1]),
        out_shape=jax.ShapeDtypeStruct((T, wkv.shape[1]), BF16),
        compiler_params=_params("parallel"),
        name="mem_kv",
    )(mem2, g[None, :], wkv.astype(BF16))


def _xattn_kernel(x_ref, g_ref, wq_ref, k_ref, v_ref, wo_ref, o_ref):
    x = x_ref[...]
    hd = x.shape[1] // X_HEADS
    q = (_dot(_rmsnorm(x, g_ref[...]).astype(BF16), wq_ref[...]) * (hd ** -0.5)).astype(BF16)
    heads = []
    for hh in range(X_HEADS):
        cols = slice(hh * hd, (hh + 1) * hd)
        s = _dot_nt(q[:, cols], k_ref[0, :, cols])
        p = jnp.exp(s - jnp.max(s, axis=-1, keepdims=True))
        den = jnp.sum(p, axis=-1, keepdims=True)
        heads.append((_dot(p.astype(BF16), v_ref[0, :, cols]) / den).astype(BF16))
    o_ref[...] = x + _dot(jnp.concatenate(heads, axis=1), wo_ref[...])


def cross_attention(h, g, wq, kv, wo, B, S):
    T, D = h.shape
    M = kv.shape[0] // B
    kv3 = kv.reshape(B, M, 2 * D)
    tm = ROW_TILE
    tps = S // tm
    return pl.pallas_call(
        _xattn_kernel,
        grid=(T // tm,),
        in_specs=[_row_spec(tm, D), _const_spec((1, D)), _const_spec((D, D)),
                  pl.BlockSpec((1, M, D), lambda i: (i // tps, 0, 0)),
                  pl.BlockSpec((1, M, D), lambda i: (i // tps, 0, 1)),
                  _const_spec((D, D))],
        out_specs=_row_spec(tm, D),
        out_shape=jax.ShapeDtypeStruct((T, D), F32),
        compiler_params=_params("parallel"),
        name="cross_attention",
    )(h, g[None, :], wq.astype(BF16), kv3, kv3, wo.astype(BF16))


def _ffn_kernel(x_ref, xp_ref, xn_ref, g_ref, wv_ref, wg_ref, cwv_ref, cwg_ref, cbv_ref, cbg_ref,
                wd_ref, gfin_ref, o_ref, uv_scr, ug_scr, *, tiles_per_seq, n_split, final_norm):
    tm = x_ref.shape[0]
    x = x_ref[...]
    prev, nxt = _halo_rows(xp_ref, xn_ref, tiles_per_seq)
    xe = _rmsnorm(jnp.concatenate([prev, x, nxt], axis=0), g_ref[...]).astype(BF16)
    fw = wv_ref.shape[1] // n_split
    base = FFN_HALO - (FFN_KERNEL - 1) // 2
    acc = x

    def conv(scr, cw_ref, cb_ref, cols):
        out = cb_ref[:, cols]
        for kk in range(FFN_KERNEL):
            out = out + cw_ref[kk:kk + 1, cols] * scr[base + kk:base + kk + tm, :]
        return out

    for c in range(n_split):
        cols = slice(c * fw, (c + 1) * fw)
        uv_scr[...] = _dot(xe, wv_ref[:, cols])
        ug_scr[...] = _dot(xe, wg_ref[:, cols])
        act = _silu(conv(ug_scr, cwg_ref, cbg_ref, cols)) * conv(uv_scr, cwv_ref, cbv_ref, cols)
        acc = acc + _dot(act.astype(BF16), wd_ref[cols, :])
    if final_norm:
        acc = _rmsnorm(acc, gfin_ref[...])
    o_ref[...] = acc


def conv_ffn(h, g, w_up, w_conv, b_conv, w_down, g_final, S, final_norm):
    T, D = h.shape
    F = w_down.shape[0]
    tm = ROW_TILE
    tiles_per_seq = S // tm
    hb = tm // FFN_HALO
    n_halo_blocks = T // FFN_HALO
    n_split = 2
    fw = F // n_split
    wu = w_up.astype(BF16)
    row = lambda v: v[None, :]
    return pl.pallas_call(
        functools.partial(_ffn_kernel, tiles_per_seq=tiles_per_seq, n_split=n_split,
                          final_norm=final_norm),
        grid=(T // tm,),
        in_specs=[_row_spec(tm, D),
                  pl.BlockSpec((FFN_HALO, D), lambda i: (jnp.maximum(i * hb - 1, 0), 0)),
                  pl.BlockSpec((FFN_HALO, D), lambda i: (jnp.minimum((i + 1) * hb, n_halo_blocks - 1), 0)),
                  _const_spec((1, D)), _const_spec((D, F)), _const_spec((D, F)),
                  _const_spec((FFN_KERNEL, F)), _const_spec((FFN_KERNEL, F)),
                  _const_spec((1, F)), _const_spec((1, F)), _const_spec((F, D)), _const_spec((1, D))],
        out_specs=_row_spec(tm, D),
        out_shape=jax.ShapeDtypeStruct((T, D), F32),
        scratch_shapes=[pltpu.VMEM((tm + 2 * FFN_HALO, fw), F32)] * 2,
        compiler_params=_params("parallel"),
        name="conv_ffn",
    )(h, h, h, row(g), wu[:, :F], wu[:, F:], w_conv[:, :F], w_conv[:, F:], row(b_conv[:F]),
      row(b_conv[F:]), w_down.astype(BF16), row(g_final))


def kernel(x, mem, positions, g_mix, g_xattn, g_mem, w_xq, w_xkv, w_xo, g_ffn, w_up, w_conv_ffn,
           b_conv_ffn, w_down, w_in_ab, w_out_ab, gla_wg2, gla_bg, gla_norm, w_in_cd, w_out_cd,
           s5_lam_re, s5_lam_im, s5_log_dt, s5_b_re, s5_b_im, s5_c_re, s5_c_im, s5_d, s5_w_glu,
           s5_b_glu, conv_w, conv_b, conv_ln_g, conv_ln_b, g_final):
    B, S, D = x.shape
    depth = g_mix.shape[0]
    assert S % ROW_TILE == 0 and S % B_CHUNK == 0 and S % S5_CHUNK == 0
    h = x.reshape(B * S, D)
    mem2 = mem.reshape(-1, D)
    cos, sin = rope_tables(positions)
    for layer in range(depth):
        i = layer // 2
        if layer % 2 == 0:
            qa, ka, va, qb, kb, vb, rb, z = in_proj_ab(h, g_mix[layer], cos, sin, w_in_ab[i])
            o_a = dilated_attention(qa, ka, va, B, S)
            o_b = gla(qb, kb, vb, rb, z, gla_wg2[i], gla_bg[i], gla_norm[i], B, S)
            h = out_proj(o_a, o_b, w_out_ab[i], h)
        else:
            u, gd, xg = in_proj_cd(h, g_mix[layer], w_in_cd[i], B, S)
            yg = s5_bidirectional(xg, (s5_lam_re[i], s5_lam_im[i], s5_log_dt[i], s5_b_re[i],
                                       s5_b_im[i], s5_c_re[i], s5_c_im[i]))
            h = cd_out(yg, u, gd, h, s5_d[i], s5_w_glu[i], s5_b_glu[i], conv_w[i], conv_b[i],
                       conv_ln_g[i], conv_ln_b[i], w_out_cd[i], S)
        kv = mem_kv(mem2, g_mem[layer], w_xkv[layer])
        h = cross_attention(h, g_xattn[layer], w_xq[layer], kv, w_xo[layer], B, S)
        h = conv_ffn(h, g_ffn[layer], w_up[layer], w_conv_ffn[layer], b_conv_ffn[layer],
                     w_down[layer], g_final, S, final_norm=(layer == depth - 1))
    return h.reshape(B, S, D)
```

```python
import functools
import math

import jax
import jax.numpy as jnp
import numpy as np
from jax import lax
from jax.experimental import pallas as pl
from jax.experimental.pallas import tpu as pltpu

F32 = jnp.float32
BF16 = jnp.bfloat16
EPS = 1e-6

LANES = 128
SUBLANES = 8
VMEM_LIMIT_BYTES = 52 * 1024 * 1024

A_HEADS, A_HEAD_DIM = 8, 64
A_PATTERNS = ((128, 1), (512, 4), (2048, 16))
A_FAR_DIL = A_PATTERNS[-1][1]
A_NEAR_REACH = max(w // 2 for w, _ in A_PATTERNS[:-1])
ATTN_TQ = 512
ROPE_THETA = 10000.0
B_HEADS, B_DK, B_DV = 4, 64, 128
B_RANK, B_TAU, B_CHUNK = 16, 16.0, 64
C_GROUP, C_NGROUPS, C_STATE = 16, 32, 64
S5_CHUNK = 16
D_KERNEL = 31
X_HEADS = 4
FFN_KERNEL = 3
NEG_BIG = -1e30

GLA_UNROLL = 4
GLA_BATCH = 8
ROW_TILE = 512
CONV_HALO = 16
FFN_HALO = 8
FFN_COLS = 256


def _params(*sem):
    return pltpu.CompilerParams(dimension_semantics=sem, vmem_limit_bytes=VMEM_LIMIT_BYTES)


def _const_spec(shape):
    zeros = (0,) * len(shape)
    return pl.BlockSpec(shape, lambda *_: zeros, pipeline_mode=pl.Buffered(1))


def _row_spec(tm, width):
    return pl.BlockSpec((tm, width), lambda i: (i, 0))


def _dot(a, b):
    return jnp.dot(a, b, preferred_element_type=F32)


def _dot_nt(a, b):
    return lax.dot_general(a, b, (((1,), (1,)), ((), ())), preferred_element_type=F32)


def _dot_tn(a, b):
    return lax.dot_general(a, b, (((0,), (0,)), ((), ())), preferred_element_type=F32)


def _rmsnorm(x, g):
    return x * lax.rsqrt(jnp.mean(x * x, axis=-1, keepdims=True) + EPS) * g


def _sigmoid(x):
    return 1.0 / (1.0 + jnp.exp(-x))


def _silu(x):
    return x * _sigmoid(x)


def _rope_table_kernel(pos_ref, invf_ref, cos_ref, sin_ref):
    ang = pos_ref[...].astype(F32) * invf_ref[...]
    lane = lax.broadcasted_iota(jnp.int32, (1, LANES), 1)
    sign = jnp.where((lane % A_HEAD_DIM) < A_HEAD_DIM // 2, -1.0, 1.0)
    cos_ref[...] = jnp.cos(ang)
    sin_ref[...] = jnp.sin(ang) * sign


def rope_tables(positions):
    T = positions.size
    tm = 1024
    inv_freq = ROPE_THETA ** (-jnp.arange(0, A_HEAD_DIM, 2, dtype=F32) / A_HEAD_DIM)
    invf = jnp.tile(inv_freq, LANES // (A_HEAD_DIM // 2))[None, :]
    return pl.pallas_call(
        _rope_table_kernel,
        grid=(T // tm,),
        in_specs=[_row_spec(tm, 1), _const_spec((1, LANES))],
        out_specs=[_row_spec(tm, LANES), _row_spec(tm, LANES)],
        out_shape=[jax.ShapeDtypeStruct((T, LANES), F32)] * 2,
        compiler_params=_params("parallel"),
        name="rope_tables",
    )(positions.reshape(T, 1), invf)


def _by_residue(val, out_ref, scr):
    tm, width = val.shape
    for k in range(width // LANES):
        scr[k] = val[:, LANES * k:LANES * (k + 1)]
    for r in range(A_FAR_DIL):
        rows = pl.ds(r, tm // A_FAR_DIL, stride=A_FAR_DIL)
        out_ref[0, r, :, :] = jnp.concatenate([scr[k, rows, :] for k in range(width // LANES)],
                                              axis=1).astype(out_ref.dtype)


def _in_ab_kernel(x_ref, g_ref, cos_ref, sin_ref, wqk_ref, wva_ref, wqkb_ref, wvr_ref, wz_ref,
                  qa_ref, ka_ref, va_ref, qb_ref, kb_ref, vb_ref, rb_ref, z_ref,
                  qr_ref, kr_ref, vr_ref, q_scr, k_scr, v_scr):
    xn = _rmsnorm(x_ref[...], g_ref[...]).astype(BF16)
    qk = _dot(xn, wqk_ref[...])
    width = qk.shape[1]
    reps = width // LANES
    cos = jnp.concatenate([cos_ref[...]] * reps, axis=1)
    sin = jnp.concatenate([sin_ref[...]] * reps, axis=1)
    half = A_HEAD_DIM // 2
    lane = lax.broadcasted_iota(jnp.int32, (1, width), 1)
    first_half = (lane % A_HEAD_DIM) < half
    partner = jnp.where(first_half, pltpu.roll(qk, width - half, axis=1), pltpu.roll(qk, half, axis=1))
    roped = qk * cos + partner * sin
    aw = width // 2
    qa = roped[:, :aw] * (A_HEAD_DIM ** -0.5)
    ka = roped[:, aw:]
    va = _dot(xn, wva_ref[...])
    qa_ref[...] = qa.astype(BF16)
    ka_ref[...] = ka.astype(BF16)
    va_ref[...] = va.astype(BF16)
    _by_residue(qa, qr_ref, q_scr)
    _by_residue(ka, kr_ref, k_scr)
    _by_residue(va, vr_ref, v_scr)
    qkb = _dot(xn, wqkb_ref[...])
    kw = qkb.shape[1] // 2
    qb_ref[...] = qkb[:, :kw] * (B_DK ** -0.5)
    kb_ref[...] = qkb[:, kw:]
    vr = _dot(xn, wvr_ref[...])
    vw = vr.shape[1] // 2
    vb_ref[...] = vr[:, :vw].astype(BF16)
    rb_ref[...] = vr[:, vw:].astype(BF16)
    z_ref[...] = _dot(xn, wz_ref[...])


def in_proj_ab(h, g, cos, sin, w_in, B, S):
    T, D = h.shape
    aw, kw, vw = A_HEADS * A_HEAD_DIM, B_HEADS * B_DK, B_HEADS * B_DV
    o = np.cumsum([0, aw, aw, aw, kw, kw, vw, vw, 2 * B_RANK])
    wb = w_in.astype(BF16)
    wqk, wva, wqkb, wvr, wz = (wb[:, o[0]:o[2]], wb[:, o[2]:o[3]], wb[:, o[3]:o[5]],
                               wb[:, o[5]:o[7]], wb[:, o[7]:o[8]])
    tm = ROW_TILE
    tps = S // tm
    outs = [(aw, BF16), (aw, BF16), (aw, BF16), (kw, F32), (kw, F32), (vw, BF16), (vw, BF16),
            (2 * B_RANK, F32)]
    res_spec = pl.BlockSpec((1, A_FAR_DIL, tm // A_FAR_DIL, aw), lambda i: (i // tps, 0, i % tps, 0))
    res_shape = jax.ShapeDtypeStruct((B, A_FAR_DIL, S // A_FAR_DIL, aw), BF16)
    return pl.pallas_call(
        _in_ab_kernel,
        grid=(T // tm,),
        in_specs=[_row_spec(tm, D), _const_spec((1, D)), _row_spec(tm, LANES), _row_spec(tm, LANES),
                  _const_spec(wqk.shape), _const_spec(wva.shape), _const_spec(wqkb.shape),
                  _const_spec(wvr.shape), _const_spec(wz.shape)],
        out_specs=[_row_spec(tm, w) for w, _ in outs] + [res_spec] * 3,
        out_shape=[jax.ShapeDtypeStruct((T, w), dt) for w, dt in outs] + [res_shape] * 3,
        scratch_shapes=[pltpu.VMEM((aw // LANES, tm, LANES), F32)] * 3,
        compiler_params=_params("parallel"),
        name="in_proj_ab",
    )(h, g[None, :], cos, sin, wqk, wva, wqkb, wvr, wz)


def _near_windows(S):
    win = ATTN_TQ + 2 * A_NEAR_REACH
    offsets, tiles = [], []
    for t0 in range(0, S, ATTN_TQ):
        ws = min(max(t0 - A_NEAR_REACH, 0), S - win)
        if ws - t0 not in offsets:
            offsets.append(ws - t0)
        tiles.append((ws, offsets.index(ws - t0)))
    return win, offsets, tiles


def _attn_kernel(q_ref, k_ref, v_ref, qr_ref, kr_ref, vr_ref, nbias_ref, fbias_ref, o_ref,
                 of_scr, lf_scr):
    S = q_ref.shape[1]
    lane = lax.broadcasted_iota(jnp.int32, (1, LANES), 1)
    first_head = lane < A_HEAD_DIM

    def softmax_pv(q, k, v, bias):
        outs, lses = [], []
        for hh in range(LANES // A_HEAD_DIM):
            qh = jnp.where((lane // A_HEAD_DIM) == hh, q, jnp.zeros_like(q))
            s = _dot_nt(qh, k) + bias
            mx = jnp.max(s, axis=-1, keepdims=True)
            p = jnp.exp(s - mx)
            den = jnp.sum(p, axis=-1, keepdims=True)
            outs.append(_dot(p.astype(BF16), v) / den)
            lses.append(mx + jnp.log(den))
        return jnp.where(first_head, outs[0], outs[1]), jnp.where(first_head, lses[0], lses[1])

    qr, kr, vr = qr_ref[0], kr_ref[0], vr_ref[0]
    outs, lses = [], []
    for hh in range(LANES // A_HEAD_DIM):
        qh = jnp.where((lane // A_HEAD_DIM) == hh, qr, jnp.zeros_like(qr))
        s = jnp.einsum('rqd,rkd->rqk', qh, kr, preferred_element_type=F32) + fbias_ref[...]
        mx = jnp.max(s, axis=-1, keepdims=True)
        p = jnp.exp(s - mx)
        den = jnp.sum(p, axis=-1, keepdims=True)
        outs.append(jnp.einsum('rqk,rkd->rqd', p.astype(BF16), vr, preferred_element_type=F32) / den)
        lses.append(mx + jnp.log(den))
    out_r = jnp.where(first_head, outs[0], outs[1])
    lse_r = jnp.where(first_head, lses[0], lses[1])
    for r in range(A_FAR_DIL):
        rows = pl.ds(r, S // A_FAR_DIL, stride=A_FAR_DIL)
        of_scr[rows, :] = out_r[r]
        lf_scr[rows, :] = lse_r[r]

    win, _, tiles = _near_windows(S)
    for i, (ws, table) in enumerate(tiles):
        rows = slice(i * ATTN_TQ, (i + 1) * ATTN_TQ)
        out_n, lse_n = softmax_pv(q_ref[0, rows, :], k_ref[0, ws:ws + win, :], v_ref[0, ws:ws + win, :],
                                  nbias_ref[table])
        out_f, lse_f = of_scr[rows, :], lf_scr[rows, :]
        mx = jnp.maximum(lse_n, lse_f)
        en, ef = jnp.exp(lse_n - mx), jnp.exp(lse_f - mx)
        o_ref[0, rows, :] = ((en * out_n + ef * out_f) / (en + ef)).astype(o_ref.dtype)


def _near_bias_kernel(o_ref, *, offsets):
    _, tq, win = o_ref.shape
    rel = (lax.broadcasted_iota(jnp.int32, (tq, win), 1)
           - lax.broadcasted_iota(jnp.int32, (tq, win), 0))
    for t, off in enumerate(offsets):
        d = rel + off
        dist = jnp.abs(d)
        count = jnp.zeros((tq, win), F32)
        for window, dil in A_PATTERNS[:-1]:
            hit = jnp.where((d & (dil - 1)) == 0, dist, 2 * A_NEAR_REACH + 1) <= window // 2
            count = count + jnp.where(hit, 1.0, 0.0)
        o_ref[t] = jnp.where(count > 0.5, jnp.log(jnp.maximum(count, 1.0)), NEG_BIG)


def _far_bias_kernel(o_ref):
    n = o_ref.shape[0]
    d = lax.broadcasted_iota(jnp.int32, (n, n), 1) - lax.broadcasted_iota(jnp.int32, (n, n), 0)
    window, dil = A_PATTERNS[-1]
    o_ref[...] = jnp.where(jnp.abs(d) <= window // (2 * dil), 0.0, NEG_BIG)


def _attention_biases(S):
    win, offsets, _ = _near_windows(S)
    near = pl.pallas_call(
        functools.partial(_near_bias_kernel, offsets=tuple(offsets)),
        out_shape=jax.ShapeDtypeStruct((len(offsets), ATTN_TQ, win), F32),
        name="near_bias",
    )()
    n = S // A_FAR_DIL
    far = pl.pallas_call(_far_bias_kernel, out_shape=jax.ShapeDtypeStruct((n, n), F32),
                         name="far_bias")()
    return near, far


def dilated_attention(qa, ka, va, qr, kr, vr, B, S):
    W = qa.shape[-1]
    q3, k3, v3 = (t.reshape(B, S, W) for t in (qa, ka, va))
    near, far = _attention_biases(S)
    seq = pl.BlockSpec((1, S, LANES), lambda b, p: (b, 0, p))
    res = pl.BlockSpec((1, A_FAR_DIL, S // A_FAR_DIL, LANES), lambda b, p: (b, 0, 0, p))
    out = pl.pallas_call(
        _attn_kernel,
        grid=(B, W // LANES),
        in_specs=[seq, seq, seq, res, res, res, _const_spec(near.shape), _const_spec(far.shape)],
        out_specs=seq,
        out_shape=jax.ShapeDtypeStruct((B, S, W), BF16),
        scratch_shapes=[pltpu.VMEM((S, LANES), F32)] * 2,
        compiler_params=_params("parallel", "parallel"),
        name="dilated_attention",
    )(q3, k3, v3, qr, kr, vr, near, far)
    return out.reshape(B * S, W)


def _log_sigmoid(x):
    return jnp.minimum(x, 0.0) - jnp.log1p(jnp.exp(-jnp.abs(x)))


def _split_bf16(x):
    hi = x.astype(BF16)
    return hi, (x - hi.astype(F32)).astype(BF16)


def _gla_kernel(q_ref, k_ref, v_ref, r_ref, z_ref, wgf_ref, wgb_ref, bgf_ref, bgb_ref, gn_ref,
                ltri_ref, utri_ref, o_ref, gf_scr, gb_scr, upd_scr, st_scr):
    S = q_ref.shape[1]
    C = B_CHUNK
    n_chunks = S // C
    kw, vw = q_ref.shape[2], v_ref.shape[2]
    ct = ltri_ref.shape[0]

    zb = z_ref[0].astype(BF16)
    for w_ref, b_ref, tri_ref, g_scr in ((wgf_ref, bgf_ref, ltri_ref, gf_scr),
                                         (wgb_ref, bgb_ref, utri_ref, gb_scr)):
        lg = _log_sigmoid(_dot(zb, w_ref[0]) + b_ref[0]) * (1.0 / B_TAU)
        hi_lo = jnp.concatenate(_split_bf16(lg), axis=1)
        for t in range(S // ct):
            rows = slice(t * ct, (t + 1) * ct)
            both = _dot(tri_ref[...], hi_lo[rows])
            g_scr[rows, :] = both[:, :kw] + both[:, kw:]

    row_v = lax.broadcasted_iota(jnp.int32, (vw, kw), 0)
    lane_k = lax.broadcasted_iota(jnp.int32, (vw, kw), 1)
    state_mask = (row_v // B_DV) == (lane_k // B_DK)
    lane1 = lax.broadcasted_iota(jnp.int32, (1, kw), 1)
    head_lane = [(lane1 // B_DK) == hh for hh in range(kw // B_DK)]
    col_v = lax.broadcasted_iota(jnp.int32, (1, vw), 1)
    head_col = [(col_v // B_DV) == hh for hh in range(vw // B_DV)]
    qi = lax.broadcasted_iota(jnp.int32, (C, kw), 0)
    kj = lax.broadcasted_iota(jnp.int32, (C, kw), 1) % C
    causal = kj <= qi

    state_mask2 = jnp.concatenate([state_mask, state_mask], axis=1)

    def chunk_rows(n):
        return pl.ds(pl.multiple_of(n * C, C), C)

    def increments(i, carry):
        for u in range(GLA_UNROLL):
            n = i * GLA_UNROLL + u
            rows = chunk_rows(n)
            gf, gb = gf_scr[rows, :], gb_scr[rows, :]
            k = k_ref[0, rows, :]
            kdec = jnp.concatenate([k * jnp.exp(gf[C - 1:C, :] - gf), k * jnp.exp(gb[0:1, :] - gb)], axis=1)
            upd_scr[n] = jnp.where(state_mask2, _dot_tn(v_ref[0, rows, :], kdec.astype(BF16)), 0.0)
        return carry

    lax.fori_loop(0, n_chunks // GLA_UNROLL, increments, 0)

    def recur(i, carry):
        sf, sb = carry
        nb = n_chunks - 1 - i
        st_scr[i, :, 0:kw] = sf.astype(BF16)
        st_scr[nb, :, kw:2 * kw] = sb.astype(BF16)
        af = jnp.exp(gf_scr[pl.ds(i * C + C - 1, 1), :])
        ab = jnp.exp(gb_scr[pl.ds(nb * C, 1), :])
        return af * sf + upd_scr[i, :, 0:kw], ab * sb + upd_scr[nb, :, kw:2 * kw]

    zero = jnp.zeros((vw, kw), F32)
    lax.fori_loop(0, n_chunks, recur, (zero, zero))

    U = GLA_BATCH

    def per_head_rows(t):
        return jnp.concatenate([jnp.where(m, t, jnp.zeros_like(t)) for m in head_lane], axis=1)

    def bdot_nt(a, b):
        return jnp.einsum('umk,unk->umn', a, b, preferred_element_type=F32)

    def outputs(i, carry):
        rows = pl.ds(pl.multiple_of(i * (U * C), U * C), U * C)
        chunks = lambda t: t.reshape(U, C, t.shape[-1])
        gf, gb = chunks(gf_scr[rows, :]), chunks(gb_scr[rows, :])
        q, k, v = chunks(q_ref[0, rows, :]), chunks(k_ref[0, rows, :]), chunks(v_ref[0, rows, :])
        qfb = jnp.concatenate([q * jnp.exp(gf), q * jnp.exp(gb)], axis=2).astype(BF16)
        kf, kb = (k * jnp.exp(-gf)).astype(BF16), (k * jnp.exp(-gb)).astype(BF16)
        att = jnp.where(causal, bdot_nt(qfb[:, :, :kw], per_head_rows(kf)),
                        bdot_nt(qfb[:, :, kw:], per_head_rows(kb)))
        v_heads = jnp.concatenate([jnp.where(m, v, jnp.zeros_like(v)) for m in head_col], axis=1)
        o = bdot_nt(qfb, st_scr[pl.ds(i * U, U)]) + jnp.einsum(
            'umk,ukn->umn', att.astype(BF16), v_heads, preferred_element_type=F32)
        normed = []
        for hh in range(vw // B_DV):
            oh = o[:, :, hh * B_DV:(hh + 1) * B_DV]
            normed.append(oh * lax.rsqrt(jnp.mean(oh * oh, axis=-1, keepdims=True) + EPS))
        gate = _silu(r_ref[0, rows, :].astype(F32))
        res = jnp.concatenate(normed, axis=2).reshape(U * C, vw) * gn_ref[0] * gate
        o_ref[0, rows, :] = res.astype(o_ref.dtype)
        return carry

    lax.fori_loop(0, n_chunks // U, outputs, 0)


def _chunk_tri(ct, chunk, upper):
    i = np.arange(ct)[:, None]
    j = np.arange(ct)[None, :]
    same = (i // chunk) == (j // chunk)
    return jnp.asarray(same & ((j >= i) if upper else (j <= i)), BF16)


def gla(qb, kb, vb, rb, z, wg2, bg, g_norm, B, S):
    npair = B_HEADS // 2
    kw, vw = 2 * B_DK, 2 * B_DV
    q3, k3 = qb.reshape(B, S, npair * kw), kb.reshape(B, S, npair * kw)
    v3, r3 = vb.reshape(B, S, npair * vw), rb.reshape(B, S, npair * vw)
    z3 = z.reshape(B, S, 2 * B_RANK)
    zero = jnp.zeros((B_RANK, B_HEADS * B_DK), F32)
    wgf = jnp.concatenate([wg2[0], zero], axis=0).astype(BF16)
    wgb = jnp.concatenate([zero, wg2[1]], axis=0).astype(BF16)
    pairs = lambda w: w.reshape(w.shape[0], npair, kw).transpose(1, 0, 2)
    ct = 256
    pair_spec = lambda shape: pl.BlockSpec((1,) + shape, lambda b, p: (p, 0, 0))
    seq_spec = lambda w: pl.BlockSpec((1, S, w), lambda b, p: (b, 0, p))
    out = pl.pallas_call(
        _gla_kernel,
        grid=(B, npair),
        in_specs=[seq_spec(kw), seq_spec(kw), seq_spec(vw), seq_spec(vw),
                  pl.BlockSpec((1, S, 2 * B_RANK), lambda b, p: (b, 0, 0)),
                  pair_spec((2 * B_RANK, kw)), pair_spec((2 * B_RANK, kw)),
                  pair_spec((1, kw)), pair_spec((1, kw)), pair_spec((1, vw)),
                  _const_spec((ct, ct)), _const_spec((ct, ct))],
        out_specs=seq_spec(vw),
        out_shape=jax.ShapeDtypeStruct((B, S, npair * vw), BF16),
        scratch_shapes=[pltpu.VMEM((S, kw), F32), pltpu.VMEM((S, kw), F32),
                        pltpu.VMEM((S // B_CHUNK, vw, 2 * kw), F32),
                        pltpu.VMEM((S // B_CHUNK, vw, 2 * kw), BF16)],
        compiler_params=_params("parallel", "parallel"),
        name="gla",
    )(q3, k3, v3, r3, z3, pairs(wgf), pairs(wgb), pairs(bg[0][None, :]), pairs(bg[1][None, :]),
      g_norm.reshape(npair, 1, vw), _chunk_tri(ct, B_CHUNK, False), _chunk_tri(ct, B_CHUNK, True))
    return out.reshape(B * S, npair * vw)


def _out_proj_kernel(a_ref, b_ref, wa_ref, wb_ref, h_ref, o_ref):
    o_ref[...] = h_ref[...] + _dot(a_ref[...], wa_ref[...]) + _dot(b_ref[...], wb_ref[...])


def out_proj(a, b, w_out, h):
    T, D = h.shape
    wa, wb = w_out[:a.shape[1]].astype(BF16), w_out[a.shape[1]:].astype(BF16)
    tm = ROW_TILE
    return pl.pallas_call(
        _out_proj_kernel,
        grid=(T // tm,),
        in_specs=[_row_spec(tm, a.shape[1]), _row_spec(tm, b.shape[1]), _const_spec(wa.shape),
                  _const_spec(wb.shape), _row_spec(tm, D)],
        out_specs=_row_spec(tm, D),
        out_shape=jax.ShapeDtypeStruct((T, D), F32),
        compiler_params=_params("parallel"),
        name="out_proj",
    )(a, b, wa, wb, h)


def _atom_transpose(x):
    rows, width = x.shape
    r = lax.broadcasted_iota(jnp.int32, (rows, width), 0)
    a = lax.broadcasted_iota(jnp.int32, (rows, width), 1) // C_GROUP
    for s in range(3):
        d = 1 << s
        rbit = (r & d) != 0
        abit = (a & d) != 0
        partner_row = jnp.where(rbit, pltpu.roll(x, d, axis=0), pltpu.roll(x, rows - d, axis=0))
        moved = jnp.where(abit, pltpu.roll(partner_row, C_GROUP * d, axis=1),
                          pltpu.roll(partner_row, width - C_GROUP * d, axis=1))
        x = jnp.where(rbit == abit, x, moved)
    return x


def _pack_groups(u, x_ref, z_scr):
    tm, width = u.shape
    z = _atom_transpose(u)
    nch = tm // S5_CHUNK
    for k in range(width // LANES):
        z_scr[k] = z[:, LANES * k:LANES * (k + 1)]
    for k in range(width // LANES):
        for g8 in range(SUBLANES):
            lo = z_scr[k, pl.ds(g8, nch, stride=S5_CHUNK), :]
            hi = z_scr[k, pl.ds(SUBLANES + g8, nch, stride=S5_CHUNK), :]
            x_ref[SUBLANES * k + g8, 0, :, :] = jnp.concatenate([lo, hi], axis=1).astype(x_ref.dtype)


def _unpack_groups(y_ref, z_scr):
    n_tiles, tm, _ = z_scr.shape
    nch = tm // S5_CHUNK
    for k in range(n_tiles):
        for g8 in range(SUBLANES):
            y = y_ref[SUBLANES * k + g8, 0, :, :]
            z_scr[k, pl.ds(g8, nch, stride=S5_CHUNK), :] = y[:, :LANES]
            z_scr[k, pl.ds(SUBLANES + g8, nch, stride=S5_CHUNK), :] = y[:, LANES:]
    return _atom_transpose(jnp.concatenate([z_scr[k] for k in range(n_tiles)], axis=1))


def _in_cd_kernel(x_ref, g_ref, wu_ref, wval_ref, wgate_ref, u_ref, gd_ref, xg_ref, z_scr):
    xn = _rmsnorm(x_ref[...], g_ref[...]).astype(BF16)
    u = _dot(xn, wu_ref[...])
    u_ref[...] = u
    _pack_groups(u, xg_ref, z_scr)
    gd_ref[...] = _dot(xn, wval_ref[...]) * _sigmoid(_dot(xn, wgate_ref[...]))


def in_proj_cd(h, g, w_in, B, S):
    T, D = h.shape
    cw = C_GROUP * C_NGROUPS
    dw = (w_in.shape[1] - cw) // 2
    wb = w_in.astype(BF16)
    tm = ROW_TILE
    tps = S // tm
    xw = S5_CHUNK * C_GROUP
    return pl.pallas_call(
        _in_cd_kernel,
        grid=(T // tm,),
        in_specs=[_row_spec(tm, D), _const_spec((1, D)), _const_spec((D, cw)), _const_spec((D, dw)),
                  _const_spec((D, dw))],
        out_specs=[_row_spec(tm, cw), _row_spec(tm, dw),
                   pl.BlockSpec((C_NGROUPS, 1, tm // S5_CHUNK, xw), lambda i: (0, i // tps, i % tps, 0))],
        out_shape=[jax.ShapeDtypeStruct((T, cw), F32), jax.ShapeDtypeStruct((T, dw), F32),
                   jax.ShapeDtypeStruct((C_NGROUPS, B, S // S5_CHUNK, xw), BF16)],
        scratch_shapes=[pltpu.VMEM((cw // LANES, tm, LANES), F32)],
        compiler_params=_params("parallel"),
        name="in_proj_cd",
    )(h, g[None, :], wb[:, :cw], wb[:, cw:cw + dw], wb[:, cw + dw:])


def _s5_operators(lam_re, lam_im, log_dt, b_re, b_im, c_re, c_im):
    hp = lax.Precision.HIGHEST
    L = S5_CHUNK
    tau = jnp.arange(L + 1, dtype=F32)
    ks, vs, ws, aL = [], [], [], []
    for d in range(2):
        lr, li = lam_re[d], lam_im[d]
        dt = jnp.exp(log_dt[d])[:, None]
        mag = jnp.exp(lr * dt)
        ab_re, ab_im = mag * jnp.cos(li * dt), mag * jnp.sin(li * dt)
        den = lr * lr + li * li
        nr = ab_re - 1.0
        f_re = (nr * lr + ab_im * li) / den
        f_im = (ab_im * lr - nr * li) / den
        bb_re = f_re[..., None] * b_re[d] - f_im[..., None] * b_im[d]
        bb_im = f_re[..., None] * b_im[d] + f_im[..., None] * b_re[d]
        pmag = jnp.exp(lr[None] * dt[None] * tau[:, None, None])
        pr = pmag * jnp.cos(li[None] * dt[None] * tau[:, None, None])
        pi = pmag * jnp.sin(li[None] * dt[None] * tau[:, None, None])
        ca_re = c_re[d][None] * pr[:, :, None, :] - c_im[d][None] * pi[:, :, None, :]
        ca_im = c_re[d][None] * pi[:, :, None, :] + c_im[d][None] * pr[:, :, None, :]
        k = (jnp.einsum('tghp,gpk->tghk', ca_re[:L], bb_re, precision=hp)
             - jnp.einsum('tghp,gpk->tghk', ca_im[:L], bb_im, precision=hp))
        ab_pow_re = pr[:, :, :, None] * bb_re[None] - pi[:, :, :, None] * bb_im[None]
        ab_pow_im = pr[:, :, :, None] * bb_im[None] + pi[:, :, :, None] * bb_re[None]
        order_v = (lambda t: t[L - 1::-1]) if d == 0 else (lambda t: t[:L])
        order_w = (lambda t: t[1:L + 1]) if d == 0 else (lambda t: t[L:0:-1])
        v = jnp.concatenate([order_v(ab_pow_re), order_v(ab_pow_im)], axis=2)
        vs.append(v.transpose(1, 0, 3, 2).reshape(v.shape[1], L * C_GROUP, 2 * C_STATE))
        w = jnp.concatenate([order_w(ca_re), -order_w(ca_im)], axis=3)
        ws.append(w.transpose(1, 3, 0, 2).reshape(w.shape[1], 2 * C_STATE, L * C_GROUP))
        ks.append(k)
        aL.append(jnp.stack([pr[L], pi[L]], axis=0))
    by_lag = jnp.concatenate([ks[1][:0:-1], (ks[0][0] + ks[1][0])[None], ks[0][1:]], axis=0)
    m = jnp.stack([by_lag[L - 1 - j:2 * L - 1 - j] for j in range(L)], axis=0)
    m = m.transpose(2, 0, 4, 1, 3).reshape(m.shape[2], L * C_GROUP, L * C_GROUP)
    return m, vs[0], vs[1], ws[0], ws[1], jnp.stack(aL, axis=0)


def _pair_blockdiag(t):
    G, r, c = t.shape
    t = t.reshape(G // 2, 2, r, c)
    z = jnp.zeros((G // 2, r, c), t.dtype)
    return jnp.concatenate([jnp.concatenate([t[:, 0], z], axis=2),
                            jnp.concatenate([z, t[:, 1]], axis=2)], axis=1)


def _s5_pair_operators(params):
    m, vf, vb, wf, wb, aL = _s5_operators(*params)
    P = C_STATE
    mm = _pair_blockdiag(m)
    vcols = [_pair_blockdiag(v[:, :, s]) for v in (vf, vb) for s in (slice(0, P), slice(P, 2 * P))]
    vv = jnp.concatenate(vcols, axis=2)
    wrows = [_pair_blockdiag(w[:, s, :]) for w in (wf, wb) for s in (slice(0, P), slice(P, 2 * P))]
    ww = jnp.concatenate(wrows, axis=1)
    G = aL.shape[2]
    aa = aL.reshape(4, G // 2, 2 * P).transpose(1, 0, 2)
    return mm.astype(BF16), vv.astype(BF16), ww.astype(BF16), aa


def _s5_kernel(x_ref, m_ref, v_ref, w_ref, a_ref, y_ref, v_scr, s_scr):
    _, B, N, half = x_ref.shape
    lw = a_ref.shape[2]
    x = jnp.concatenate([x_ref[0].reshape(B * N, half), x_ref[1].reshape(B * N, half)], axis=1)
    v = _dot(x, v_ref[0])
    for p in range(4):
        v_scr[p] = v[:, lw * p:lw * (p + 1)]
    a = a_ref[0]
    afr, afi, abr, abi = a[0:1], a[1:2], a[2:3], a[3:4]

    def step(n, carry):
        fr, fi, br, bi = carry
        rf = pl.ds(n, B, stride=N)
        rb = pl.ds(N - 1 - n, B, stride=N)
        s_scr[0, rf, :] = fr
        s_scr[1, rf, :] = fi
        s_scr[2, rb, :] = br
        s_scr[3, rb, :] = bi
        return (afr * fr - afi * fi + v_scr[0, rf, :], afr * fi + afi * fr + v_scr[1, rf, :],
                abr * br - abi * bi + v_scr[2, rb, :], abr * bi + abi * br + v_scr[3, rb, :])

    zero = jnp.zeros((B, lw), F32)
    lax.fori_loop(0, N, step, (zero, zero, zero, zero))
    s = jnp.concatenate([s_scr[p] for p in range(4)], axis=1).astype(BF16)
    y = _dot(x, m_ref[0]) + _dot(s, w_ref[0])
    y_ref[0] = y[:, :half].reshape(B, N, half)
    y_ref[1] = y[:, half:].reshape(B, N, half)


def s5_bidirectional(xg, params):
    G, B, N, half = xg.shape
    mm, vv, ww, aa = _s5_pair_operators(params)
    pw = 2 * half
    sw = vv.shape[2]
    op_spec = lambda shape: pl.BlockSpec((1,) + shape, lambda p: (p, 0, 0))
    pair_spec = pl.BlockSpec((2, B, N, half), lambda p: (p, 0, 0, 0))
    return pl.pallas_call(
        _s5_kernel,
        grid=(G // 2,),
        in_specs=[pair_spec, op_spec((pw, pw)), op_spec((pw, sw)), op_spec((sw, pw)),
                  op_spec((4, sw // 4))],
        out_specs=pair_spec,
        out_shape=jax.ShapeDtypeStruct((G, B, N, half), F32),
        scratch_shapes=[pltpu.VMEM((4, B * N, sw // 4), F32)] * 2,
        compiler_params=_params("parallel"),
        name="s5",
    )(xg, mm, vv, ww, aa)


def _halo_rows(prev_ref, next_ref, tiles_per_seq):
    t = pl.program_id(0) % tiles_per_seq
    prev = jnp.where(t > 0, prev_ref[...], 0.0)
    nxt = jnp.where(t < tiles_per_seq - 1, next_ref[...], 0.0)
    return prev, nxt


def _cd_out_kernel(y_ref, u_ref, gd_ref, gdp_ref, gdn_ref, h_ref, dskip_ref, wglu_ref, bglu_ref,
                   cw_ref, cb_ref, lng_ref, lnb_ref, wc_ref, wd_ref, o_ref, ext_scr, z_scr, sh_scr, *,
                   tiles_per_seq):
    tm = u_ref.shape[0]
    z = jax.nn.gelu(_unpack_groups(y_ref, z_scr) + dskip_ref[...] * u_ref[...])
    o_c = z * _sigmoid(_dot(z.astype(BF16), wglu_ref[...]) + bglu_ref[...])
    prev, nxt = _halo_rows(gdp_ref, gdn_ref, tiles_per_seq)
    ext_scr[0:CONV_HALO, :] = prev
    ext_scr[CONV_HALO:CONV_HALO + tm, :] = gd_ref[...]
    ext_scr[CONV_HALO + tm:CONV_HALO + tm + CONV_HALO, :] = nxt
    n_rows = tm + 2 * CONV_HALO - SUBLANES
    for r in range(1, SUBLANES):
        sh_scr[r - 1] = ext_scr[r:r + n_rows, :]
    base = CONV_HALO - (D_KERNEL - 1) // 2
    acc = jnp.zeros(gd_ref.shape, F32) + cb_ref[...]
    for kk in range(D_KERNEL):
        q, r = divmod(base + kk, SUBLANES)
        rows = slice(q * SUBLANES, q * SUBLANES + tm)
        tap = ext_scr[rows, :] if r == 0 else sh_scr[r - 1, rows, :]
        acc = acc + cw_ref[kk:kk + 1, :] * tap
    mu = jnp.mean(acc, axis=-1, keepdims=True)
    cen = acc - mu
    var = jnp.mean(cen * cen, axis=-1, keepdims=True)
    o_d = _silu(cen * lax.rsqrt(var + EPS) * lng_ref[...] + lnb_ref[...])
    o_ref[...] = (h_ref[...] + _dot(o_c.astype(BF16), wc_ref[...])
                  + _dot(o_d.astype(BF16), wd_ref[...]))


def cd_out(yg, u, gd, h, d_skip, w_glu, b_glu, conv_w, conv_b, ln_g, ln_b, w_out, S):
    T, D = h.shape
    cw, dw = u.shape[1], gd.shape[1]
    tm = ROW_TILE
    tiles_per_seq = tps = S // tm
    hb = tm // CONV_HALO
    n_halo_blocks = T // CONV_HALO
    row = lambda v: v[None, :]
    return pl.pallas_call(
        functools.partial(_cd_out_kernel, tiles_per_seq=tiles_per_seq),
        grid=(T // tm,),
        in_specs=[pl.BlockSpec((yg.shape[0], 1, tm // S5_CHUNK, yg.shape[3]),
                               lambda i: (0, i // tps, i % tps, 0)),
                  _row_spec(tm, cw), _row_spec(tm, dw),
                  pl.BlockSpec((CONV_HALO, dw), lambda i: (jnp.maximum(i * hb - 1, 0), 0)),
                  pl.BlockSpec((CONV_HALO, dw), lambda i: (jnp.minimum((i + 1) * hb, n_halo_blocks - 1), 0)),
                  _row_spec(tm, D), _const_spec((1, cw)), _const_spec((cw, cw)), _const_spec((1, cw)),
                  _const_spec((D_KERNEL, dw)), _const_spec((1, dw)), _const_spec((1, dw)),
                  _const_spec((1, dw)), _const_spec((cw, D)), _const_spec((dw, D))],
        out_specs=_row_spec(tm, D),
        out_shape=jax.ShapeDtypeStruct((T, D), F32),
        scratch_shapes=[pltpu.VMEM((tm + 2 * CONV_HALO, dw), F32), pltpu.VMEM((cw // LANES, tm, LANES), F32),
                        pltpu.VMEM((SUBLANES - 1, tm + 2 * CONV_HALO - SUBLANES, dw), F32)],
        compiler_params=_params("parallel"),
        name="cd_out",
    )(yg, u, gd, gd, gd, h, row(d_skip), w_glu.astype(BF16), row(b_glu), conv_w, row(conv_b),
      row(ln_g), row(ln_b), w_out[:cw].astype(BF16), w_out[cw:].astype(BF16))


def _kv_kernel(m_ref, g_ref, w_ref, o_ref):
    o_ref[...] = _dot(_rmsnorm(m_ref[...], g_ref[...]).astype(BF16), w_ref[...]).astype(o_ref.dtype)


def mem_kv(mem2, g, wkv):
    T, D = mem2.shape
    tm = ROW_TILE
    return pl.pallas_call(
        _kv_kernel,
        grid=(T // tm,),
        in_specs=[_row_spec(tm, D), _const_spec((1, D)), _const_spec(wkv.shape)],
        out_specs=_row_spec(tm, wkv.shape[1]),
        out_shape=jax.ShapeDtypeStruct((T, wkv.shape[1]), BF16),
        compiler_params=_params("parallel"),
        name="mem_kv",
    )(mem2, g[None, :], wkv.astype(BF16))


def _xattn_kernel(x_ref, g_ref, wq_ref, k_ref, v_ref, wo_ref, o_ref):
    x = x_ref[...]
    hd = x.shape[1] // X_HEADS
    q = (_dot(_rmsnorm(x, g_ref[...]).astype(BF16), wq_ref[...]) * (hd ** -0.5)).astype(BF16)
    heads = []
    for hh in range(X_HEADS):
        cols = slice(hh * hd, (hh + 1) * hd)
        s = _dot_nt(q[:, cols], k_ref[0, :, cols])
        p = jnp.exp(s - jnp.max(s, axis=-1, keepdims=True))
        den = jnp.sum(p, axis=-1, keepdims=True)
        heads.append((_dot(p.astype(BF16), v_ref[0, :, cols]) / den).astype(BF16))
    o_ref[...] = x + _dot(jnp.concatenate(heads, axis=1), wo_ref[...])


def cross_attention(h, g, wq, kv, wo, B, S):
    T, D = h.shape
    M = kv.shape[0] // B
    kv3 = kv.reshape(B, M, 2 * D)
    tm = ROW_TILE
    tps = S // tm
    return pl.pallas_call(
        _xattn_kernel,
        grid=(T // tm,),
        in_specs=[_row_spec(tm, D), _const_spec((1, D)), _const_spec((D, D)),
                  pl.BlockSpec((1, M, D), lambda i: (i // tps, 0, 0)),
                  pl.BlockSpec((1, M, D), lambda i: (i // tps, 0, 1)),
                  _const_spec((D, D))],
        out_specs=_row_spec(tm, D),
        out_shape=jax.ShapeDtypeStruct((T, D), F32),
        compiler_params=_params("parallel"),
        name="cross_attention",
    )(h, g[None, :], wq.astype(BF16), kv3, kv3, wo.astype(BF16))


def _ffn_kernel(x_ref, xp_ref, xn_ref, g_ref, wv_ref, wg_ref, cwv_ref, cwg_ref, cbv_ref, cbg_ref,
                wd_ref, gfin_ref, o_ref, act_scr, *, tiles_per_seq, n_split, final_norm):
    tm = x_ref.shape[0]
    x = x_ref[...]
    prev, nxt = _halo_rows(xp_ref, xn_ref, tiles_per_seq)
    xe = _rmsnorm(jnp.concatenate([prev, x, nxt], axis=0), g_ref[...]).astype(BF16)
    rows_ext = xe.shape[0]
    fw = wv_ref.shape[1] // n_split
    mid = slice(FFN_HALO, FFN_HALO + tm)

    def conv(u, cw_ref, cb_ref, cols):
        before = pltpu.roll(u, 1, axis=0)[mid]
        after = pltpu.roll(u, rows_ext - 1, axis=0)[mid]
        return (cb_ref[:, cols] + cw_ref[0:1, cols] * before + cw_ref[1:2, cols] * u[mid]
                + cw_ref[2:3, cols] * after)

    for c in range(n_split):
        cols = slice(c * fw, (c + 1) * fw)
        gate = conv(_dot(xe, wg_ref[:, cols]), cwg_ref, cbg_ref, cols)
        val = conv(_dot(xe, wv_ref[:, cols]), cwv_ref, cbv_ref, cols)
        act_scr[:, cols] = (_silu(gate) * val).astype(BF16)
    acc = x + _dot(act_scr[...], wd_ref[...])
    if final_norm:
        acc = _rmsnorm(acc, gfin_ref[...])
    o_ref[...] = acc


def conv_ffn(h, g, w_up, w_conv, b_conv, w_down, g_final, S, final_norm):
    T, D = h.shape
    F = w_down.shape[0]
    tm = ROW_TILE
    tiles_per_seq = S // tm
    hb = tm // FFN_HALO
    n_halo_blocks = T // FFN_HALO
    n_split = F // FFN_COLS
    wu = w_up.astype(BF16)
    row = lambda v: v[None, :]
    return pl.pallas_call(
        functools.partial(_ffn_kernel, tiles_per_seq=tiles_per_seq, n_split=n_split,
                          final_norm=final_norm),
        grid=(T // tm,),
        in_specs=[_row_spec(tm, D),
                  pl.BlockSpec((FFN_HALO, D), lambda i: (jnp.maximum(i * hb - 1, 0), 0)),
                  pl.BlockSpec((FFN_HALO, D), lambda i: (jnp.minimum((i + 1) * hb, n_halo_blocks - 1), 0)),
                  _const_spec((1, D)), _const_spec((D, F)), _const_spec((D, F)),
                  _const_spec((FFN_KERNEL, F)), _const_spec((FFN_KERNEL, F)),
                  _const_spec((1, F)), _const_spec((1, F)), _const_spec((F, D)), _const_spec((1, D))],
        out_specs=_row_spec(tm, D),
        out_shape=jax.ShapeDtypeStruct((T, D), F32),
        scratch_shapes=[pltpu.VMEM((tm, F), BF16)],
        compiler_params=_params("parallel"),
        name="conv_ffn",
    )(h, h, h, row(g), wu[:, :F], wu[:, F:], w_conv[:, :F], w_conv[:, F:], row(b_conv[:F]),
      row(b_conv[F:]), w_down.astype(BF16), row(g_final))


def kernel(x, mem, positions, g_mix, g_xattn, g_mem, w_xq, w_xkv, w_xo, g_ffn, w_up, w_conv_ffn,
           b_conv_ffn, w_down, w_in_ab, w_out_ab, gla_wg2, gla_bg, gla_norm, w_in_cd, w_out_cd,
           s5_lam_re, s5_lam_im, s5_log_dt, s5_b_re, s5_b_im, s5_c_re, s5_c_im, s5_d, s5_w_glu,
           s5_b_glu, conv_w, conv_b, conv_ln_g, conv_ln_b, g_final):
    B, S, D = x.shape
    depth = g_mix.shape[0]
    assert S % ROW_TILE == 0 and S % B_CHUNK == 0 and S % S5_CHUNK == 0
    h = x.reshape(B * S, D)
    mem2 = mem.reshape(-1, D)
    cos, sin = rope_tables(positions)
    for layer in range(depth):
        i = layer // 2
        if layer % 2 == 0:
            qa, ka, va, qb, kb, vb, rb, z, qr, kr, vr = in_proj_ab(h, g_mix[layer], cos, sin,
                                                                   w_in_ab[i], B, S)
            o_a = dilated_attention(qa, ka, va, qr, kr, vr, B, S)
            o_b = gla(qb, kb, vb, rb, z, gla_wg2[i], gla_bg[i], gla_norm[i], B, S)
            h = out_proj(o_a, o_b, w_out_ab[i], h)
        else:
            u, gd, xg = in_proj_cd(h, g_mix[layer], w_in_cd[i], B, S)
            yg = s5_bidirectional(xg, (s5_lam_re[i], s5_lam_im[i], s5_log_dt[i], s5_b_re[i],
                                       s5_b_im[i], s5_c_re[i], s5_c_im[i]))
            h = cd_out(yg, u, gd, h, s5_d[i], s5_w_glu[i], s5_b_glu[i], conv_w[i], conv_b[i],
                       conv_ln_g[i], conv_ln_b[i], w_out_cd[i], S)
        kv = mem_kv(mem2, g_mem[layer], w_xkv[layer])
        h = cross_attention(h, g_xattn[layer], w_xq[layer], kv, w_xo[layer], B, S)
        h = conv_ffn(h, g_ffn[layer], w_up[layer], w_conv_ffn[layer], b_conv_ffn[layer],
                     w_down[layer], g_final, S, final_norm=(layer == depth - 1))
    return h.reshape(B, S, D)
```

```python
import functools
import math

import jax
import jax.numpy as jnp
import numpy as np
from jax import lax
from jax.experimental import pallas as pl
from jax.experimental.pallas import tpu as pltpu

F32 = jnp.float32
BF16 = jnp.bfloat16
EPS = 1e-6

LANES = 128
SUBLANES = 8
VMEM_LIMIT_BYTES = 52 * 1024 * 1024

A_HEADS, A_HEAD_DIM = 8, 64
A_PATTERNS = ((128, 1), (512, 4), (2048, 16))
A_FAR_DIL = A_PATTERNS[-1][1]
A_NEAR_REACH = max(w // 2 for w, _ in A_PATTERNS[:-1])
ATTN_TQ = 256
ROPE_THETA = 10000.0
B_HEADS, B_DK, B_DV = 4, 64, 128
B_RANK, B_TAU, B_CHUNK = 16, 16.0, 64
C_GROUP, C_NGROUPS, C_STATE = 16, 32, 64
S5_CHUNK = 16
D_KERNEL = 31
X_HEADS = 4
FFN_KERNEL = 3
NEG_BIG = -1e30
LOG2E = math.log2(math.e)

GLA_UNROLL = 4
GLA_BATCH = 8
ROW_TILE = 512
CONV_HALO = 16
FFN_HALO = 8
FFN_COLS = 256


def _params(*sem):
    return pltpu.CompilerParams(dimension_semantics=sem, vmem_limit_bytes=VMEM_LIMIT_BYTES)


def _const_spec(shape):
    zeros = (0,) * len(shape)
    return pl.BlockSpec(shape, lambda *_: zeros, pipeline_mode=pl.Buffered(1))


def _row_spec(tm, width):
    return pl.BlockSpec((tm, width), lambda i: (i, 0))


def _dot(a, b):
    return jnp.dot(a, b, preferred_element_type=F32)


def _dot_nt(a, b):
    return lax.dot_general(a, b, (((1,), (1,)), ((), ())), preferred_element_type=F32)


def _dot_tn(a, b):
    return lax.dot_general(a, b, (((0,), (0,)), ((), ())), preferred_element_type=F32)


def _rmsnorm(x, g):
    return x * lax.rsqrt(jnp.mean(x * x, axis=-1, keepdims=True) + EPS) * g


def _sigmoid(x):
    return 1.0 / (1.0 + jnp.exp(-x))


def _silu(x):
    return x * _sigmoid(x)


def _rope_table_kernel(pos_ref, invf_ref, cos_ref, sin_ref):
    ang = pos_ref[...].astype(F32) * invf_ref[...]
    lane = lax.broadcasted_iota(jnp.int32, (1, LANES), 1)
    sign = jnp.where((lane % A_HEAD_DIM) < A_HEAD_DIM // 2, -1.0, 1.0)
    cos_ref[...] = jnp.cos(ang)
    sin_ref[...] = jnp.sin(ang) * sign


def rope_tables(positions):
    T = positions.size
    tm = 1024
    inv_freq = ROPE_THETA ** (-jnp.arange(0, A_HEAD_DIM, 2, dtype=F32) / A_HEAD_DIM)
    invf = jnp.tile(inv_freq, LANES // (A_HEAD_DIM // 2))[None, :]
    return pl.pallas_call(
        _rope_table_kernel,
        grid=(T // tm,),
        in_specs=[_row_spec(tm, 1), _const_spec((1, LANES))],
        out_specs=[_row_spec(tm, LANES), _row_spec(tm, LANES)],
        out_shape=[jax.ShapeDtypeStruct((T, LANES), F32)] * 2,
        compiler_params=_params("parallel"),
        name="rope_tables",
    )(positions.reshape(T, 1), invf)


def _by_residue(val, out_ref, scr):
    tm, width = val.shape
    for k in range(width // LANES):
        scr[k] = val[:, LANES * k:LANES * (k + 1)]
    for r in range(A_FAR_DIL):
        rows = pl.ds(r, tm // A_FAR_DIL, stride=A_FAR_DIL)
        out_ref[0, r, :, :] = jnp.concatenate([scr[k, rows, :] for k in range(width // LANES)],
                                              axis=1).astype(out_ref.dtype)


def _in_ab_kernel(x_ref, g_ref, cos_ref, sin_ref, wqk_ref, wva_ref, wqkb_ref, wvr_ref, wz_ref,
                  qa_ref, ka_ref, va_ref, qb_ref, kb_ref, vb_ref, rb_ref, z_ref,
                  qr_ref, kr_ref, vr_ref, q_scr, k_scr, v_scr):
    xn = _rmsnorm(x_ref[...], g_ref[...]).astype(BF16)
    qk = _dot(xn, wqk_ref[...])
    width = qk.shape[1]
    reps = width // LANES
    cos = jnp.concatenate([cos_ref[...]] * reps, axis=1)
    sin = jnp.concatenate([sin_ref[...]] * reps, axis=1)
    half = A_HEAD_DIM // 2
    lane = lax.broadcasted_iota(jnp.int32, (1, width), 1)
    first_half = (lane % A_HEAD_DIM) < half
    partner = jnp.where(first_half, pltpu.roll(qk, width - half, axis=1), pltpu.roll(qk, half, axis=1))
    roped = qk * cos + partner * sin
    aw = width // 2
    qa = roped[:, :aw] * (A_HEAD_DIM ** -0.5 * LOG2E)
    ka = roped[:, aw:]
    va = _dot(xn, wva_ref[...])
    qa_ref[...] = qa.astype(BF16)
    ka_ref[...] = ka.astype(BF16)
    va_ref[...] = va.astype(BF16)
    _by_residue(qa, qr_ref, q_scr)
    _by_residue(ka, kr_ref, k_scr)
    _by_residue(va, vr_ref, v_scr)
    qkb = _dot(xn, wqkb_ref[...])
    kw = qkb.shape[1] // 2
    qb_ref[...] = qkb[:, :kw] * (B_DK ** -0.5)
    kb_ref[...] = qkb[:, kw:]
    vr = _dot(xn, wvr_ref[...])
    vw = vr.shape[1] // 2
    vb_ref[...] = vr[:, :vw].astype(BF16)
    rb_ref[...] = vr[:, vw:].astype(BF16)
    z_ref[...] = _dot(xn, wz_ref[...])


def in_proj_ab(h, g, cos, sin, w_in, B, S):
    T, D = h.shape
    aw, kw, vw = A_HEADS * A_HEAD_DIM, B_HEADS * B_DK, B_HEADS * B_DV
    o = np.cumsum([0, aw, aw, aw, kw, kw, vw, vw, 2 * B_RANK])
    wb = w_in.astype(BF16)
    wqk, wva, wqkb, wvr, wz = (wb[:, o[0]:o[2]], wb[:, o[2]:o[3]], wb[:, o[3]:o[5]],
                               wb[:, o[5]:o[7]], wb[:, o[7]:o[8]])
    tm = ROW_TILE
    tps = S // tm
    outs = [(aw, BF16), (aw, BF16), (aw, BF16), (kw, F32), (kw, F32), (vw, BF16), (vw, BF16),
            (2 * B_RANK, F32)]
    res_spec = pl.BlockSpec((1, A_FAR_DIL, tm // A_FAR_DIL, aw), lambda i: (i // tps, 0, i % tps, 0))
    res_shape = jax.ShapeDtypeStruct((B, A_FAR_DIL, S // A_FAR_DIL, aw), BF16)
    return pl.pallas_call(
        _in_ab_kernel,
        grid=(T // tm,),
        in_specs=[_row_spec(tm, D), _const_spec((1, D)), _row_spec(tm, LANES), _row_spec(tm, LANES),
                  _const_spec(wqk.shape), _const_spec(wva.shape), _const_spec(wqkb.shape),
                  _const_spec(wvr.shape), _const_spec(wz.shape)],
        out_specs=[_row_spec(tm, w) for w, _ in outs] + [res_spec] * 3,
        out_shape=[jax.ShapeDtypeStruct((T, w), dt) for w, dt in outs] + [res_shape] * 3,
        scratch_shapes=[pltpu.VMEM((aw // LANES, tm, LANES), F32)] * 3,
        compiler_params=_params("parallel"),
        name="in_proj_ab",
    )(h, g[None, :], cos, sin, wqk, wva, wqkb, wvr, wz)


def _near_windows(S):
    win = ATTN_TQ + 2 * A_NEAR_REACH
    return win, [min(max(t0 - A_NEAR_REACH, 0), S - win) for t0 in range(0, S, ATTN_TQ)]


def _attn_kernel(q_ref, k_ref, v_ref, qr_ref, kr_ref, vr_ref, nbias_ref, fbias_ref, o_ref,
                 of_scr, lf_scr):
    S = q_ref.shape[1]
    lane = lax.broadcasted_iota(jnp.int32, (1, LANES), 1)
    first_head = lane < A_HEAD_DIM

    def softmax_pv(q, k, v, bias):
        outs, lses = [], []
        for hh in range(LANES // A_HEAD_DIM):
            qh = jnp.where((lane // A_HEAD_DIM) == hh, q, jnp.zeros_like(q))
            s = jnp.einsum('rqd,rkd->rqk', qh, k, preferred_element_type=F32) + bias
            mx = jnp.max(s, axis=-1, keepdims=True)
            p = jnp.exp2(s - mx)
            den = jnp.sum(p, axis=-1, keepdims=True)
            outs.append(jnp.einsum('rqk,rkd->rqd', p.astype(BF16), v, preferred_element_type=F32) / den)
            lses.append(mx + jnp.log2(den))
        return jnp.where(first_head, outs[0], outs[1]), jnp.where(first_head, lses[0], lses[1])

    out_r, lse_r = softmax_pv(qr_ref[0], kr_ref[0], vr_ref[0], fbias_ref[...])
    for r in range(A_FAR_DIL):
        rows = pl.ds(r, S // A_FAR_DIL, stride=A_FAR_DIL)
        of_scr[rows, :] = out_r[r]
        lf_scr[rows, :] = lse_r[r]

    win, starts = _near_windows(S)
    windows = lambda ref: jnp.stack([ref[0, ws:ws + win, :] for ws in starts], axis=0)
    out_n, lse_n = softmax_pv(q_ref[0].reshape(len(starts), ATTN_TQ, LANES), windows(k_ref),
                              windows(v_ref), nbias_ref[...])
    out_n, lse_n = out_n.reshape(S, LANES), lse_n.reshape(S, LANES)

    out_f, lse_f = of_scr[...], lf_scr[...]
    mx = jnp.maximum(lse_n, lse_f)
    en, ef = jnp.exp2(lse_n - mx), jnp.exp2(lse_f - mx)
    o_ref[0] = ((en * out_n + ef * out_f) / (en + ef)).astype(o_ref.dtype)


def _near_bias_kernel(o_ref, *, starts):
    _, tq, win = o_ref.shape
    rel = (lax.broadcasted_iota(jnp.int32, (tq, win), 1)
           - lax.broadcasted_iota(jnp.int32, (tq, win), 0))
    for t, ws in enumerate(starts):
        d = rel + (ws - t * tq)
        dist = jnp.abs(d)
        count = jnp.zeros((tq, win), F32)
        for window, dil in A_PATTERNS[:-1]:
            hit = jnp.where((d & (dil - 1)) == 0, dist, 2 * A_NEAR_REACH + 1) <= window // 2
            count = count + jnp.where(hit, 1.0, 0.0)
        o_ref[t] = jnp.where(count > 0.5, jnp.log2(jnp.maximum(count, 1.0)), NEG_BIG)


def _far_bias_kernel(o_ref):
    n = o_ref.shape[0]
    d = lax.broadcasted_iota(jnp.int32, (n, n), 1) - lax.broadcasted_iota(jnp.int32, (n, n), 0)
    window, dil = A_PATTERNS[-1]
    o_ref[...] = jnp.where(jnp.abs(d) <= window // (2 * dil), 0.0, NEG_BIG)


def _attention_biases(S):
    win, starts = _near_windows(S)
    near = pl.pallas_call(
        functools.partial(_near_bias_kernel, starts=tuple(starts)),
        out_shape=jax.ShapeDtypeStruct((len(starts), ATTN_TQ, win), F32),
        name="near_bias",
    )()
    n = S // A_FAR_DIL
    far = pl.pallas_call(_far_bias_kernel, out_shape=jax.ShapeDtypeStruct((n, n), F32),
                         name="far_bias")()
    return near, far


def dilated_attention(qa, ka, va, qr, kr, vr, B, S):
    W = qa.shape[-1]
    q3, k3, v3 = (t.reshape(B, S, W) for t in (qa, ka, va))
    near, far = _attention_biases(S)
    seq = pl.BlockSpec((1, S, LANES), lambda b, p: (b, 0, p))
    res = pl.BlockSpec((1, A_FAR_DIL, S // A_FAR_DIL, LANES), lambda b, p: (b, 0, 0, p))
    out = pl.pallas_call(
        _attn_kernel,
        grid=(B, W // LANES),
        in_specs=[seq, seq, seq, res, res, res, _const_spec(near.shape), _const_spec(far.shape)],
        out_specs=seq,
        out_shape=jax.ShapeDtypeStruct((B, S, W), BF16),
        scratch_shapes=[pltpu.VMEM((S, LANES), F32)] * 2,
        compiler_params=_params("parallel", "parallel"),
        name="dilated_attention",
    )(q3, k3, v3, qr, kr, vr, near, far)
    return out.reshape(B * S, W)


def _log_sigmoid(x):
    return jnp.minimum(x, 0.0) - jnp.log1p(jnp.exp(-jnp.abs(x)))


def _split_bf16(x):
    hi = x.astype(BF16)
    return hi, (x - hi.astype(F32)).astype(BF16)


def _gla_kernel(q_ref, k_ref, v_ref, r_ref, z_ref, wgf_ref, wgb_ref, bgf_ref, bgb_ref, gn_ref,
                ltri_ref, utri_ref, o_ref, gf_scr, gb_scr, upd_scr, st_scr):
    S = q_ref.shape[1]
    C = B_CHUNK
    n_chunks = S // C
    kw, vw = q_ref.shape[2], v_ref.shape[2]
    ct = ltri_ref.shape[0]

    zb = z_ref[0].astype(BF16)
    for w_ref, b_ref, tri_ref, g_scr in ((wgf_ref, bgf_ref, ltri_ref, gf_scr),
                                         (wgb_ref, bgb_ref, utri_ref, gb_scr)):
        lg = _log_sigmoid(_dot(zb, w_ref[0]) + b_ref[0]) * (1.0 / B_TAU)
        hi_lo = jnp.concatenate(_split_bf16(lg), axis=1)
        for t in range(S // ct):
            rows = slice(t * ct, (t + 1) * ct)
            both = _dot(tri_ref[...], hi_lo[rows])
            g_scr[rows, :] = both[:, :kw] + both[:, kw:]

    row_v = lax.broadcasted_iota(jnp.int32, (vw, kw), 0)
    lane_k = lax.broadcasted_iota(jnp.int32, (vw, kw), 1)
    state_mask = (row_v // B_DV) == (lane_k // B_DK)
    lane1 = lax.broadcasted_iota(jnp.int32, (1, kw), 1)
    head_lane = [(lane1 // B_DK) == hh for hh in range(kw // B_DK)]
    col_v = lax.broadcasted_iota(jnp.int32, (1, vw), 1)
    head_col = [(col_v // B_DV) == hh for hh in range(vw // B_DV)]
    qi = lax.broadcasted_iota(jnp.int32, (C, kw), 0)
    kj = lax.broadcasted_iota(jnp.int32, (C, kw), 1) % C
    causal = kj <= qi

    state_mask2 = jnp.concatenate([state_mask, state_mask], axis=1)

    def chunk_rows(n):
        return pl.ds(pl.multiple_of(n * C, C), C)

    def increments(i, carry):
        for u in range(GLA_UNROLL):
            n = i * GLA_UNROLL + u
            rows = chunk_rows(n)
            gf, gb = gf_scr[rows, :], gb_scr[rows, :]
            k = k_ref[0, rows, :]
            kdec = jnp.concatenate([k * jnp.exp(gf[C - 1:C, :] - gf), k * jnp.exp(gb[0:1, :] - gb)], axis=1)
            upd_scr[n] = jnp.where(state_mask2, _dot_tn(v_ref[0, rows, :], kdec.astype(BF16)), 0.0)
        return carry

    lax.fori_loop(0, n_chunks // GLA_UNROLL, increments, 0)

    def recur(i, carry):
        sf, sb = carry
        nb = n_chunks - 1 - i
        st_scr[i, :, 0:kw] = sf.astype(BF16)
        st_scr[nb, :, kw:2 * kw] = sb.astype(BF16)
        af = jnp.exp(gf_scr[pl.ds(i * C + C - 1, 1), :])
        ab = jnp.exp(gb_scr[pl.ds(nb * C, 1), :])
        return af * sf + upd_scr[i, :, 0:kw], ab * sb + upd_scr[nb, :, kw:2 * kw]

    zero = jnp.zeros((vw, kw), F32)
    lax.fori_loop(0, n_chunks, recur, (zero, zero))

    U = GLA_BATCH

    def per_head_rows(t):
        return jnp.concatenate([jnp.where(m, t, jnp.zeros_like(t)) for m in head_lane], axis=1)

    def bdot_nt(a, b):
        return jnp.einsum('umk,unk->umn', a, b, preferred_element_type=F32)

    def outputs(i, carry):
        rows = pl.ds(pl.multiple_of(i * (U * C), U * C), U * C)
        chunks = lambda t: t.reshape(U, C, t.shape[-1])
        gf, gb = chunks(gf_scr[rows, :]), chunks(gb_scr[rows, :])
        q, k, v = chunks(q_ref[0, rows, :]), chunks(k_ref[0, rows, :]), chunks(v_ref[0, rows, :])
        qfb = jnp.concatenate([q * jnp.exp(gf), q * jnp.exp(gb)], axis=2).astype(BF16)
        kf, kb = (k * jnp.exp(-gf)).astype(BF16), (k * jnp.exp(-gb)).astype(BF16)
        att = jnp.where(causal, bdot_nt(qfb[:, :, :kw], per_head_rows(kf)),
                        bdot_nt(qfb[:, :, kw:], per_head_rows(kb)))
        v_heads = jnp.concatenate([jnp.where(m, v, jnp.zeros_like(v)) for m in head_col], axis=1)
        o = bdot_nt(qfb, st_scr[pl.ds(i * U, U)]) + jnp.einsum(
            'umk,ukn->umn', att.astype(BF16), v_heads, preferred_element_type=F32)
        normed = []
        for hh in range(vw // B_DV):
            oh = o[:, :, hh * B_DV:(hh + 1) * B_DV]
            normed.append(oh * lax.rsqrt(jnp.mean(oh * oh, axis=-1, keepdims=True) + EPS))
        gate = _silu(r_ref[0, rows, :].astype(F32))
        res = jnp.concatenate(normed, axis=2).reshape(U * C, vw) * gn_ref[0] * gate
        o_ref[0, rows, :] = res.astype(o_ref.dtype)
        return carry

    lax.fori_loop(0, n_chunks // U, outputs, 0)


def _chunk_tri(ct, chunk, upper):
    i = np.arange(ct)[:, None]
    j = np.arange(ct)[None, :]
    same = (i // chunk) == (j // chunk)
    return jnp.asarray(same & ((j >= i) if upper else (j <= i)), BF16)


def gla(qb, kb, vb, rb, z, wg2, bg, g_norm, B, S):
    npair = B_HEADS // 2
    kw, vw = 2 * B_DK, 2 * B_DV
    q3, k3 = qb.reshape(B, S, npair * kw), kb.reshape(B, S, npair * kw)
    v3, r3 = vb.reshape(B, S, npair * vw), rb.reshape(B, S, npair * vw)
    z3 = z.reshape(B, S, 2 * B_RANK)
    zero = jnp.zeros((B_RANK, B_HEADS * B_DK), F32)
    wgf = jnp.concatenate([wg2[0], zero], axis=0).astype(BF16)
    wgb = jnp.concatenate([zero, wg2[1]], axis=0).astype(BF16)
    pairs = lambda w: w.reshape(w.shape[0], npair, kw).transpose(1, 0, 2)
    ct = 256
    pair_spec = lambda shape: pl.BlockSpec((1,) + shape, lambda b, p: (p, 0, 0))
    seq_spec = lambda w: pl.BlockSpec((1, S, w), lambda b, p: (b, 0, p))
    out = pl.pallas_call(
        _gla_kernel,
        grid=(B, npair),
        in_specs=[seq_spec(kw), seq_spec(kw), seq_spec(vw), seq_spec(vw),
                  pl.BlockSpec((1, S, 2 * B_RANK), lambda b, p: (b, 0, 0)),
                  pair_spec((2 * B_RANK, kw)), pair_spec((2 * B_RANK, kw)),
                  pair_spec((1, kw)), pair_spec((1, kw)), pair_spec((1, vw)),
                  _const_spec((ct, ct)), _const_spec((ct, ct))],
        out_specs=seq_spec(vw),
        out_shape=jax.ShapeDtypeStruct((B, S, npair * vw), BF16),
        scratch_shapes=[pltpu.VMEM((S, kw), F32), pltpu.VMEM((S, kw), F32),
                        pltpu.VMEM((S // B_CHUNK, vw, 2 * kw), F32),
                        pltpu.VMEM((S // B_CHUNK, vw, 2 * kw), BF16)],
        compiler_params=_params("parallel", "parallel"),
        name="gla",
    )(q3, k3, v3, r3, z3, pairs(wgf), pairs(wgb), pairs(bg[0][None, :]), pairs(bg[1][None, :]),
      g_norm.reshape(npair, 1, vw), _chunk_tri(ct, B_CHUNK, False), _chunk_tri(ct, B_CHUNK, True))
    return out.reshape(B * S, npair * vw)


def _atom_transpose(x):
    rows, width = x.shape
    r = lax.broadcasted_iota(jnp.int32, (rows, width), 0)
    a = lax.broadcasted_iota(jnp.int32, (rows, width), 1) // C_GROUP
    for s in range(3):
        d = 1 << s
        rbit = (r & d) != 0
        abit = (a & d) != 0
        partner_row = jnp.where(rbit, pltpu.roll(x, d, axis=0), pltpu.roll(x, rows - d, axis=0))
        moved = jnp.where(abit, pltpu.roll(partner_row, C_GROUP * d, axis=1),
                          pltpu.roll(partner_row, width - C_GROUP * d, axis=1))
        x = jnp.where(rbit == abit, x, moved)
    return x


def _pack_groups(u, x_ref, z_scr):
    tm, width = u.shape
    z = _atom_transpose(u)
    nch = tm // S5_CHUNK
    for k in range(width // LANES):
        z_scr[k] = z[:, LANES * k:LANES * (k + 1)]
    for k in range(width // LANES):
        for g8 in range(SUBLANES):
            lo = z_scr[k, pl.ds(g8, nch, stride=S5_CHUNK), :]
            hi = z_scr[k, pl.ds(SUBLANES + g8, nch, stride=S5_CHUNK), :]
            x_ref[SUBLANES * k + g8, 0, :, :] = jnp.concatenate([lo, hi], axis=1).astype(x_ref.dtype)


def _unpack_groups(y_ref, z_scr):
    n_tiles, tm, _ = z_scr.shape
    nch = tm // S5_CHUNK
    for k in range(n_tiles):
        for g8 in range(SUBLANES):
            y = y_ref[SUBLANES * k + g8, 0, :, :]
            z_scr[k, pl.ds(g8, nch, stride=S5_CHUNK), :] = y[:, :LANES]
            z_scr[k, pl.ds(SUBLANES + g8, nch, stride=S5_CHUNK), :] = y[:, LANES:]
    return _atom_transpose(jnp.concatenate([z_scr[k] for k in range(n_tiles)], axis=1))


def _in_cd_kernel(x_ref, g_ref, wu_ref, wval_ref, wgate_ref, u_ref, gd_ref, xg_ref, z_scr):
    xn = _rmsnorm(x_ref[...], g_ref[...]).astype(BF16)
    u = _dot(xn, wu_ref[...])
    u_ref[...] = u
    _pack_groups(u, xg_ref, z_scr)
    gd_ref[...] = _dot(xn, wval_ref[...]) * _sigmoid(_dot(xn, wgate_ref[...]))


def in_proj_cd(h, g, w_in, B, S):
    T, D = h.shape
    cw = C_GROUP * C_NGROUPS
    dw = (w_in.shape[1] - cw) // 2
    wb = w_in.astype(BF16)
    tm = ROW_TILE
    tps = S // tm
    xw = S5_CHUNK * C_GROUP
    return pl.pallas_call(
        _in_cd_kernel,
        grid=(T // tm,),
        in_specs=[_row_spec(tm, D), _const_spec((1, D)), _const_spec((D, cw)), _const_spec((D, dw)),
                  _const_spec((D, dw))],
        out_specs=[_row_spec(tm, cw), _row_spec(tm, dw),
                   pl.BlockSpec((C_NGROUPS, 1, tm // S5_CHUNK, xw), lambda i: (0, i // tps, i % tps, 0))],
        out_shape=[jax.ShapeDtypeStruct((T, cw), F32), jax.ShapeDtypeStruct((T, dw), F32),
                   jax.ShapeDtypeStruct((C_NGROUPS, B, S // S5_CHUNK, xw), BF16)],
        scratch_shapes=[pltpu.VMEM((cw // LANES, tm, LANES), F32)],
        compiler_params=_params("parallel"),
        name="in_proj_cd",
    )(h, g[None, :], wb[:, :cw], wb[:, cw:cw + dw], wb[:, cw + dw:])


def _s5_operators(lam_re, lam_im, log_dt, b_re, b_im, c_re, c_im):
    hp = lax.Precision.HIGHEST
    L = S5_CHUNK
    tau = jnp.arange(L + 1, dtype=F32)
    ks, vs, ws, aL = [], [], [], []
    for d in range(2):
        lr, li = lam_re[d], lam_im[d]
        dt = jnp.exp(log_dt[d])[:, None]
        mag = jnp.exp(lr * dt)
        ab_re, ab_im = mag * jnp.cos(li * dt), mag * jnp.sin(li * dt)
        den = lr * lr + li * li
        nr = ab_re - 1.0
        f_re = (nr * lr + ab_im * li) / den
        f_im = (ab_im * lr - nr * li) / den
        bb_re = f_re[..., None] * b_re[d] - f_im[..., None] * b_im[d]
        bb_im = f_re[..., None] * b_im[d] + f_im[..., None] * b_re[d]
        pmag = jnp.exp(lr[None] * dt[None] * tau[:, None, None])
        pr = pmag * jnp.cos(li[None] * dt[None] * tau[:, None, None])
        pi = pmag * jnp.sin(li[None] * dt[None] * tau[:, None, None])
        ca_re = c_re[d][None] * pr[:, :, None, :] - c_im[d][None] * pi[:, :, None, :]
        ca_im = c_re[d][None] * pi[:, :, None, :] + c_im[d][None] * pr[:, :, None, :]
        k = (jnp.einsum('tghp,gpk->tghk', ca_re[:L], bb_re, precision=hp)
             - jnp.einsum('tghp,gpk->tghk', ca_im[:L], bb_im, precision=hp))
        ab_pow_re = pr[:, :, :, None] * bb_re[None] - pi[:, :, :, None] * bb_im[None]
        ab_pow_im = pr[:, :, :, None] * bb_im[None] + pi[:, :, :, None] * bb_re[None]
        order_v = (lambda t: t[L - 1::-1]) if d == 0 else (lambda t: t[:L])
        order_w = (lambda t: t[1:L + 1]) if d == 0 else (lambda t: t[L:0:-1])
        v = jnp.concatenate([order_v(ab_pow_re), order_v(ab_pow_im)], axis=2)
        vs.append(v.transpose(1, 0, 3, 2).reshape(v.shape[1], L * C_GROUP, 2 * C_STATE))
        w = jnp.concatenate([order_w(ca_re), -order_w(ca_im)], axis=3)
        ws.append(w.transpose(1, 3, 0, 2).reshape(w.shape[1], 2 * C_STATE, L * C_GROUP))
        ks.append(k)
        aL.append(jnp.stack([pr[L], pi[L]], axis=0))
    by_lag = jnp.concatenate([ks[1][:0:-1], (ks[0][0] + ks[1][0])[None], ks[0][1:]], axis=0)
    m = jnp.stack([by_lag[L - 1 - j:2 * L - 1 - j] for j in range(L)], axis=0)
    m = m.transpose(2, 0, 4, 1, 3).reshape(m.shape[2], L * C_GROUP, L * C_GROUP)
    return m, vs[0], vs[1], ws[0], ws[1], jnp.stack(aL, axis=0)


def _pair_blockdiag(t):
    G, r, c = t.shape
    t = t.reshape(G // 2, 2, r, c)
    z = jnp.zeros((G // 2, r, c), t.dtype)
    return jnp.concatenate([jnp.concatenate([t[:, 0], z], axis=2),
                            jnp.concatenate([z, t[:, 1]], axis=2)], axis=1)


def _s5_pair_operators(params):
    fold = lambda t: jnp.moveaxis(t, 1, 0).reshape((2, t.shape[0] * t.shape[2]) + t.shape[3:])
    m, vf, vb, wf, wb, aL = _s5_operators(*(fold(t) for t in params))
    P = C_STATE
    mm = _pair_blockdiag(m)
    vcols = [_pair_blockdiag(v[:, :, s]) for v in (vf, vb) for s in (slice(0, P), slice(P, 2 * P))]
    vv = jnp.concatenate(vcols, axis=2)
    wrows = [_pair_blockdiag(w[:, s, :]) for w in (wf, wb) for s in (slice(0, P), slice(P, 2 * P))]
    ww = jnp.concatenate(wrows, axis=1)
    G = aL.shape[2]
    aa = aL.reshape(4, G // 2, 2 * P).transpose(1, 0, 2)
    return mm.astype(BF16), vv.astype(BF16), ww.astype(BF16), aa


def _s5_kernel(x_ref, m_ref, v_ref, w_ref, a_ref, y_ref, v_scr, s_scr):
    _, B, N, half = x_ref.shape
    lw = a_ref.shape[2]
    x = jnp.concatenate([x_ref[0].reshape(B * N, half), x_ref[1].reshape(B * N, half)], axis=1)
    v = _dot(x, v_ref[0])
    for p in range(4):
        v_scr[p] = v[:, lw * p:lw * (p + 1)]
    a = a_ref[0]
    afr, afi, abr, abi = a[0:1], a[1:2], a[2:3], a[3:4]

    def step(n, carry):
        fr, fi, br, bi = carry
        rf = pl.ds(n, B, stride=N)
        rb = pl.ds(N - 1 - n, B, stride=N)
        s_scr[0, rf, :] = fr
        s_scr[1, rf, :] = fi
        s_scr[2, rb, :] = br
        s_scr[3, rb, :] = bi
        return (afr * fr - afi * fi + v_scr[0, rf, :], afr * fi + afi * fr + v_scr[1, rf, :],
                abr * br - abi * bi + v_scr[2, rb, :], abr * bi + abi * br + v_scr[3, rb, :])

    zero = jnp.zeros((B, lw), F32)
    lax.fori_loop(0, N, step, (zero, zero, zero, zero))
    s = jnp.concatenate([s_scr[p] for p in range(4)], axis=1).astype(BF16)
    y = _dot(x, m_ref[0]) + _dot(s, w_ref[0])
    y_ref[0] = y[:, :half].reshape(B, N, half)
    y_ref[1] = y[:, half:].reshape(B, N, half)


def s5_bidirectional(xg, operators, layer):
    G, B, N, half = xg.shape
    mm, vv, ww, aa = operators
    pw = 2 * half
    sw = vv.shape[2]
    first = layer * (G // 2)
    op_spec = lambda shape: pl.BlockSpec((1,) + shape, lambda p: (first + p, 0, 0))
    pair_spec = pl.BlockSpec((2, B, N, half), lambda p: (p, 0, 0, 0))
    return pl.pallas_call(
        _s5_kernel,
        grid=(G // 2,),
        in_specs=[pair_spec, op_spec((pw, pw)), op_spec((pw, sw)), op_spec((sw, pw)),
                  op_spec((4, sw // 4))],
        out_specs=pair_spec,
        out_shape=jax.ShapeDtypeStruct((G, B, N, half), F32),
        scratch_shapes=[pltpu.VMEM((4, B * N, sw // 4), F32)] * 2,
        compiler_params=_params("parallel"),
        name="s5",
    )(xg, mm, vv, ww, aa)


def _halo_rows(prev_ref, next_ref, tiles_per_seq):
    t = pl.program_id(0) % tiles_per_seq
    prev = jnp.where(t > 0, prev_ref[...], 0.0)
    nxt = jnp.where(t < tiles_per_seq - 1, next_ref[...], 0.0)
    return prev, nxt


def _cd_out_kernel(y_ref, u_ref, gd_ref, gdp_ref, gdn_ref, h_ref, dskip_ref, wglu_ref, bglu_ref,
                   cw_ref, cb_ref, lng_ref, lnb_ref, wc_ref, wd_ref, o_ref, ext_scr, z_scr, sh_scr, *,
                   tiles_per_seq):
    tm = u_ref.shape[0]
    z = jax.nn.gelu(_unpack_groups(y_ref, z_scr) + dskip_ref[...] * u_ref[...])
    o_c = z * _sigmoid(_dot(z.astype(BF16), wglu_ref[...]) + bglu_ref[...])
    prev, nxt = _halo_rows(gdp_ref, gdn_ref, tiles_per_seq)
    ext_scr[0:CONV_HALO, :] = prev
    ext_scr[CONV_HALO:CONV_HALO + tm, :] = gd_ref[...]
    ext_scr[CONV_HALO + tm:CONV_HALO + tm + CONV_HALO, :] = nxt
    n_rows = tm + 2 * CONV_HALO - SUBLANES
    for r in range(1, SUBLANES):
        sh_scr[r - 1] = ext_scr[r:r + n_rows, :]
    base = CONV_HALO - (D_KERNEL - 1) // 2
    acc = jnp.zeros(gd_ref.shape, F32) + cb_ref[...]
    for kk in range(D_KERNEL):
        q, r = divmod(base + kk, SUBLANES)
        rows = slice(q * SUBLANES, q * SUBLANES + tm)
        tap = ext_scr[rows, :] if r == 0 else sh_scr[r - 1, rows, :]
        acc = acc + cw_ref[kk:kk + 1, :] * tap
    mu = jnp.mean(acc, axis=-1, keepdims=True)
    cen = acc - mu
    var = jnp.mean(cen * cen, axis=-1, keepdims=True)
    o_d = _silu(cen * lax.rsqrt(var + EPS) * lng_ref[...] + lnb_ref[...])
    o_ref[...] = (h_ref[...] + _dot(o_c.astype(BF16), wc_ref[...])
                  + _dot(o_d.astype(BF16), wd_ref[...]))


def cd_out(yg, u, gd, h, d_skip, w_glu, b_glu, conv_w, conv_b, ln_g, ln_b, w_out, S):
    T, D = h.shape
    cw, dw = u.shape[1], gd.shape[1]
    tm = ROW_TILE
    tiles_per_seq = tps = S // tm
    hb = tm // CONV_HALO
    n_halo_blocks = T // CONV_HALO
    row = lambda v: v[None, :]
    return pl.pallas_call(
        functools.partial(_cd_out_kernel, tiles_per_seq=tiles_per_seq),
        grid=(T // tm,),
        in_specs=[pl.BlockSpec((yg.shape[0], 1, tm // S5_CHUNK, yg.shape[3]),
                               lambda i: (0, i // tps, i % tps, 0)),
                  _row_spec(tm, cw), _row_spec(tm, dw),
                  pl.BlockSpec((CONV_HALO, dw), lambda i: (jnp.maximum(i * hb - 1, 0), 0)),
                  pl.BlockSpec((CONV_HALO, dw), lambda i: (jnp.minimum((i + 1) * hb, n_halo_blocks - 1), 0)),
                  _row_spec(tm, D), _const_spec((1, cw)), _const_spec((cw, cw)), _const_spec((1, cw)),
                  _const_spec((D_KERNEL, dw)), _const_spec((1, dw)), _const_spec((1, dw)),
                  _const_spec((1, dw)), _const_spec((cw, D)), _const_spec((dw, D))],
        out_specs=_row_spec(tm, D),
        out_shape=jax.ShapeDtypeStruct((T, D), F32),
        scratch_shapes=[pltpu.VMEM((tm + 2 * CONV_HALO, dw), F32), pltpu.VMEM((cw // LANES, tm, LANES), F32),
                        pltpu.VMEM((SUBLANES - 1, tm + 2 * CONV_HALO - SUBLANES, dw), F32)],
        compiler_params=_params("parallel"),
        name="cd_out",
    )(yg, u, gd, gd, gd, h, row(d_skip), w_glu.astype(BF16), row(b_glu), conv_w, row(conv_b),
      row(ln_g), row(ln_b), w_out[:cw].astype(BF16), w_out[cw:].astype(BF16))


def _kv_kernel(m_ref, g_ref, w_ref, o_ref):
    o_ref[...] = _dot(_rmsnorm(m_ref[...], g_ref[...]).astype(BF16), w_ref[...]).astype(o_ref.dtype)


def mem_kv(mem2, g, wkv):
    T, D = mem2.shape
    tm = ROW_TILE
    return pl.pallas_call(
        _kv_kernel,
        grid=(T // tm,),
        in_specs=[_row_spec(tm, D), _const_spec((1, D)), _const_spec(wkv.shape)],
        out_specs=_row_spec(tm, wkv.shape[1]),
        out_shape=jax.ShapeDtypeStruct((T, wkv.shape[1]), BF16),
        compiler_params=_params("parallel"),
        name="mem_kv",
    )(mem2, g[None, :], wkv.astype(BF16))


def _xattn_kernel(x_ref, g_ref, wq_ref, k_ref, v_ref, wo_ref, *rest, mixer_out):
    if mixer_out:
        a_ref, b_ref, wa_ref, wb_ref, o_ref = rest
        x = x_ref[...] + _dot(a_ref[...], wa_ref[...]) + _dot(b_ref[...], wb_ref[...])
    else:
        (o_ref,) = rest
        x = x_ref[...]
    hd = x.shape[1] // X_HEADS
    q = (_dot(_rmsnorm(x, g_ref[...]).astype(BF16), wq_ref[...]) * (hd ** -0.5)).astype(BF16)
    heads = []
    for hh in range(X_HEADS):
        cols = slice(hh * hd, (hh + 1) * hd)
        s = _dot_nt(q[:, cols], k_ref[0, :, cols])
        p = jnp.exp(s - jnp.max(s, axis=-1, keepdims=True))
        den = jnp.sum(p, axis=-1, keepdims=True)
        heads.append((_dot(p.astype(BF16), v_ref[0, :, cols]) / den).astype(BF16))
    o_ref[...] = x + _dot(jnp.concatenate(heads, axis=1), wo_ref[...])


def cross_attention(h, g, wq, kv, wo, B, S, mixer=None):
    T, D = h.shape
    M = kv.shape[0] // B
    kv3 = kv.reshape(B, M, 2 * D)
    tm = ROW_TILE
    tps = S // tm
    in_specs = [_row_spec(tm, D), _const_spec((1, D)), _const_spec((D, D)),
                pl.BlockSpec((1, M, D), lambda i: (i // tps, 0, 0)),
                pl.BlockSpec((1, M, D), lambda i: (i // tps, 0, 1)),
                _const_spec((D, D))]
    args = [h, g[None, :], wq.astype(BF16), kv3, kv3, wo.astype(BF16)]
    if mixer is not None:
        a, b, w_out = mixer
        wa, wb = w_out[:a.shape[1]].astype(BF16), w_out[a.shape[1]:].astype(BF16)
        in_specs += [_row_spec(tm, a.shape[1]), _row_spec(tm, b.shape[1]), _const_spec(wa.shape),
                     _const_spec(wb.shape)]
        args += [a, b, wa, wb]
    return pl.pallas_call(
        functools.partial(_xattn_kernel, mixer_out=mixer is not None),
        grid=(T // tm,),
        in_specs=in_specs,
        out_specs=_row_spec(tm, D),
        out_shape=jax.ShapeDtypeStruct((T, D), F32),
        compiler_params=_params("parallel"),
        name="cross_attention",
    )(*args)


def _ffn_kernel(x_ref, xp_ref, xn_ref, g_ref, wv_ref, wg_ref, cwv_ref, cwg_ref, cbv_ref, cbg_ref,
                wd_ref, gfin_ref, o_ref, act_scr, *, tiles_per_seq, n_split, final_norm):
    tm = x_ref.shape[0]
    x = x_ref[...]
    prev, nxt = _halo_rows(xp_ref, xn_ref, tiles_per_seq)
    xe = _rmsnorm(jnp.concatenate([prev, x, nxt], axis=0), g_ref[...]).astype(BF16)
    rows_ext = xe.shape[0]
    fw = wv_ref.shape[1] // n_split
    mid = slice(FFN_HALO, FFN_HALO + tm)

    def conv(u, cw_ref, cb_ref, cols):
        before = pltpu.roll(u, 1, axis=0)[mid]
        after = pltpu.roll(u, rows_ext - 1, axis=0)[mid]
        return (cb_ref[:, cols] + cw_ref[0:1, cols] * before + cw_ref[1:2, cols] * u[mid]
                + cw_ref[2:3, cols] * after)

    for c in range(n_split):
        cols = slice(c * fw, (c + 1) * fw)
        gate = conv(_dot(xe, wg_ref[:, cols]), cwg_ref, cbg_ref, cols)
        val = conv(_dot(xe, wv_ref[:, cols]), cwv_ref, cbv_ref, cols)
        act_scr[:, cols] = (_silu(gate) * val).astype(BF16)
    acc = x + _dot(act_scr[...], wd_ref[...])
    if final_norm:
        acc = _rmsnorm(acc, gfin_ref[...])
    o_ref[...] = acc


def conv_ffn(h, g, w_up, w_conv, b_conv, w_down, g_final, S, final_norm):
    T, D = h.shape
    F = w_down.shape[0]
    tm = ROW_TILE
    tiles_per_seq = S // tm
    hb = tm // FFN_HALO
    n_halo_blocks = T // FFN_HALO
    n_split = F // FFN_COLS
    wu = w_up.astype(BF16)
    row = lambda v: v[None, :]
    return pl.pallas_call(
        functools.partial(_ffn_kernel, tiles_per_seq=tiles_per_seq, n_split=n_split,
                          final_norm=final_norm),
        grid=(T // tm,),
        in_specs=[_row_spec(tm, D),
                  pl.BlockSpec((FFN_HALO, D), lambda i: (jnp.maximum(i * hb - 1, 0), 0)),
                  pl.BlockSpec((FFN_HALO, D), lambda i: (jnp.minimum((i + 1) * hb, n_halo_blocks - 1), 0)),
                  _const_spec((1, D)), _const_spec((D, F)), _const_spec((D, F)),
                  _const_spec((FFN_KERNEL, F)), _const_spec((FFN_KERNEL, F)),
                  _const_spec((1, F)), _const_spec((1, F)), _const_spec((F, D)), _const_spec((1, D))],
        out_specs=_row_spec(tm, D),
        out_shape=jax.ShapeDtypeStruct((T, D), F32),
        scratch_shapes=[pltpu.VMEM((tm, F), BF16)],
        compiler_params=_params("parallel"),
        name="conv_ffn",
    )(h, h, h, row(g), wu[:, :F], wu[:, F:], w_conv[:, :F], w_conv[:, F:], row(b_conv[:F]),
      row(b_conv[F:]), w_down.astype(BF16), row(g_final))


def kernel(x, mem, positions, g_mix, g_xattn, g_mem, w_xq, w_xkv, w_xo, g_ffn, w_up, w_conv_ffn,
           b_conv_ffn, w_down, w_in_ab, w_out_ab, gla_wg2, gla_bg, gla_norm, w_in_cd, w_out_cd,
           s5_lam_re, s5_lam_im, s5_log_dt, s5_b_re, s5_b_im, s5_c_re, s5_c_im, s5_d, s5_w_glu,
           s5_b_glu, conv_w, conv_b, conv_ln_g, conv_ln_b, g_final):
    B, S, D = x.shape
    depth = g_mix.shape[0]
    assert S % ROW_TILE == 0 and S % B_CHUNK == 0 and S % S5_CHUNK == 0
    h = x.reshape(B * S, D)
    mem2 = mem.reshape(-1, D)
    cos, sin = rope_tables(positions)
    s5_ops = _s5_pair_operators((s5_lam_re, s5_lam_im, s5_log_dt, s5_b_re, s5_b_im, s5_c_re, s5_c_im))
    for layer in range(depth):
        i = layer // 2
        if layer % 2 == 0:
            qa, ka, va, qb, kb, vb, rb, z, qr, kr, vr = in_proj_ab(h, g_mix[layer], cos, sin,
                                                                   w_in_ab[i], B, S)
            o_a = dilated_attention(qa, ka, va, qr, kr, vr, B, S)
            o_b = gla(qb, kb, vb, rb, z, gla_wg2[i], gla_bg[i], gla_norm[i], B, S)
            mixer = (o_a, o_b, w_out_ab[i])
        else:
            mixer = None
            u, gd, xg = in_proj_cd(h, g_mix[layer], w_in_cd[i], B, S)
            yg = s5_bidirectional(xg, s5_ops, i)
            h = cd_out(yg, u, gd, h, s5_d[i], s5_w_glu[i], s5_b_glu[i], conv_w[i], conv_b[i],
                       conv_ln_g[i], conv_ln_b[i], w_out_cd[i], S)
        kv = mem_kv(mem2, g_mem[layer], w_xkv[layer])
        h = cross_attention(h, g_xattn[layer], w_xq[layer], kv, w_xo[layer], B, S, mixer)
        h = conv_ffn(h, g_ffn[layer], w_up[layer], w_conv_ffn[layer], b_conv_ffn[layer],
                     w_down[layer], g_final, S, final_norm=(layer == depth - 1))
    return h.reshape(B, S, D)
```

```python
import functools
import math

import jax
import jax.numpy as jnp
import numpy as np
from jax import lax
from jax.experimental import pallas as pl
from jax.experimental.pallas import tpu as pltpu

F32 = jnp.float32
BF16 = jnp.bfloat16
EPS = 1e-6

LANES = 128
SUBLANES = 8
VMEM_LIMIT_BYTES = 52 * 1024 * 1024

A_HEADS, A_HEAD_DIM = 8, 64
A_PATTERNS = ((128, 1), (512, 4), (2048, 16))
A_FAR_DIL = A_PATTERNS[-1][1]
A_NEAR_REACH = max(w // 2 for w, _ in A_PATTERNS[:-1])
ATTN_TQ = 256
ROPE_THETA = 10000.0
B_HEADS, B_DK, B_DV = 4, 64, 128
B_RANK, B_TAU, B_CHUNK = 16, 16.0, 64
C_GROUP, C_NGROUPS, C_STATE = 16, 32, 64
S5_CHUNK = 16
D_KERNEL = 31
X_HEADS = 4
FFN_KERNEL = 3
NEG_BIG = -1e30
LOG2E = math.log2(math.e)

GLA_UNROLL = 4
GLA_BATCH = 8
ROW_TILE = 512
CONV_HALO = 16
FFN_HALO = 8
FFN_COLS = 256


def _params(*sem):
    return pltpu.CompilerParams(dimension_semantics=sem, vmem_limit_bytes=VMEM_LIMIT_BYTES)


def _const_spec(shape):
    zeros = (0,) * len(shape)
    return pl.BlockSpec(shape, lambda *_: zeros, pipeline_mode=pl.Buffered(1))


def _layer_spec(shape, layer, col_block=0):
    return pl.BlockSpec((1,) + shape, lambda *_: (layer, 0, col_block), pipeline_mode=pl.Buffered(1))


def _row_spec(tm, width):
    return pl.BlockSpec((tm, width), lambda i: (i, 0))


def _dot(a, b):
    return jnp.dot(a, b, preferred_element_type=F32)


def _dot_nt(a, b):
    return lax.dot_general(a, b, (((1,), (1,)), ((), ())), preferred_element_type=F32)


def _dot_tn(a, b):
    return lax.dot_general(a, b, (((0,), (0,)), ((), ())), preferred_element_type=F32)


def _rmsnorm(x, g):
    return x * lax.rsqrt(jnp.mean(x * x, axis=-1, keepdims=True) + EPS) * g


def _sigmoid(x):
    return 1.0 / (1.0 + jnp.exp(-x))


def _silu(x):
    return x * _sigmoid(x)


def _rope_table_kernel(pos_ref, invf_ref, cos_ref, sin_ref):
    ang = pos_ref[...].astype(F32) * invf_ref[...]
    lane = lax.broadcasted_iota(jnp.int32, (1, LANES), 1)
    sign = jnp.where((lane % A_HEAD_DIM) < A_HEAD_DIM // 2, -1.0, 1.0)
    cos_ref[...] = jnp.cos(ang)
    sin_ref[...] = jnp.sin(ang) * sign


def rope_tables(positions):
    T = positions.size
    tm = 1024
    inv_freq = ROPE_THETA ** (-jnp.arange(0, A_HEAD_DIM, 2, dtype=F32) / A_HEAD_DIM)
    invf = jnp.tile(inv_freq, LANES // (A_HEAD_DIM // 2))[None, :]
    return pl.pallas_call(
        _rope_table_kernel,
        grid=(T // tm,),
        in_specs=[_row_spec(tm, 1), _const_spec((1, LANES))],
        out_specs=[_row_spec(tm, LANES), _row_spec(tm, LANES)],
        out_shape=[jax.ShapeDtypeStruct((T, LANES), F32)] * 2,
        compiler_params=_params("parallel"),
        name="rope_tables",
    )(positions.reshape(T, 1), invf)


def _by_residue(val, out_ref, scr):
    tm, width = val.shape
    for k in range(width // LANES):
        scr[k] = val[:, LANES * k:LANES * (k + 1)]
    for r in range(A_FAR_DIL):
        rows = pl.ds(r, tm // A_FAR_DIL, stride=A_FAR_DIL)
        out_ref[0, r, :, :] = jnp.concatenate([scr[k, rows, :] for k in range(width // LANES)],
                                              axis=1).astype(out_ref.dtype)


def _in_ab_kernel(x_ref, g_ref, cos_ref, sin_ref, wqk_ref, wva_ref, wqkb_ref, wvr_ref, wz_ref,
                  qa_ref, ka_ref, va_ref, qb_ref, kb_ref, vb_ref, rb_ref, z_ref,
                  qr_ref, kr_ref, vr_ref, q_scr, k_scr, v_scr):
    xn = _rmsnorm(x_ref[...], g_ref[...]).astype(BF16)
    qk = _dot(xn, wqk_ref[...])
    width = qk.shape[1]
    reps = width // LANES
    cos = jnp.concatenate([cos_ref[...]] * reps, axis=1)
    sin = jnp.concatenate([sin_ref[...]] * reps, axis=1)
    half = A_HEAD_DIM // 2
    lane = lax.broadcasted_iota(jnp.int32, (1, width), 1)
    first_half = (lane % A_HEAD_DIM) < half
    partner = jnp.where(first_half, pltpu.roll(qk, width - half, axis=1), pltpu.roll(qk, half, axis=1))
    roped = qk * cos + partner * sin
    aw = width // 2
    qa = roped[:, :aw] * (A_HEAD_DIM ** -0.5 * LOG2E)
    ka = roped[:, aw:]
    va = _dot(xn, wva_ref[...])
    qa_ref[...] = qa.astype(BF16)
    ka_ref[...] = ka.astype(BF16)
    va_ref[...] = va.astype(BF16)
    _by_residue(qa, qr_ref, q_scr)
    _by_residue(ka, kr_ref, k_scr)
    _by_residue(va, vr_ref, v_scr)
    qkb = _dot(xn, wqkb_ref[...])
    kw = qkb.shape[1] // 2
    qb_ref[...] = qkb[:, :kw] * (B_DK ** -0.5)
    kb_ref[...] = qkb[:, kw:]
    vr = _dot(xn, wvr_ref[...])
    vw = vr.shape[1] // 2
    vb_ref[...] = vr[:, :vw].astype(BF16)
    rb_ref[...] = vr[:, vw:].astype(BF16)
    z_ref[...] = _dot(xn, wz_ref[...])


def in_proj_ab(h, g, cos, sin, w_in, B, S):
    T, D = h.shape
    aw, kw, vw = A_HEADS * A_HEAD_DIM, B_HEADS * B_DK, B_HEADS * B_DV
    o = np.cumsum([0, aw, aw, aw, kw, kw, vw, vw, 2 * B_RANK])
    wb = w_in.astype(BF16)
    wqk, wva, wqkb, wvr, wz = (wb[:, o[0]:o[2]], wb[:, o[2]:o[3]], wb[:, o[3]:o[5]],
                               wb[:, o[5]:o[7]], wb[:, o[7]:o[8]])
    tm = ROW_TILE
    tps = S // tm
    outs = [(aw, BF16), (aw, BF16), (aw, BF16), (kw, F32), (kw, F32), (vw, BF16), (vw, BF16),
            (2 * B_RANK, F32)]
    res_spec = pl.BlockSpec((1, A_FAR_DIL, tm // A_FAR_DIL, aw), lambda i: (i // tps, 0, i % tps, 0))
    res_shape = jax.ShapeDtypeStruct((B, A_FAR_DIL, S // A_FAR_DIL, aw), BF16)
    return pl.pallas_call(
        _in_ab_kernel,
        grid=(T // tm,),
        in_specs=[_row_spec(tm, D), _const_spec((1, D)), _row_spec(tm, LANES), _row_spec(tm, LANES),
                  _const_spec(wqk.shape), _const_spec(wva.shape), _const_spec(wqkb.shape),
                  _const_spec(wvr.shape), _const_spec(wz.shape)],
        out_specs=[_row_spec(tm, w) for w, _ in outs] + [res_spec] * 3,
        out_shape=[jax.ShapeDtypeStruct((T, w), dt) for w, dt in outs] + [res_shape] * 3,
        scratch_shapes=[pltpu.VMEM((aw // LANES, tm, LANES), F32)] * 3,
        compiler_params=_params("parallel"),
        name="in_proj_ab",
    )(h, g[None, :], cos, sin, wqk, wva, wqkb, wvr, wz)


def _near_windows(S):
    win = ATTN_TQ + 2 * A_NEAR_REACH
    return win, [min(max(t0 - A_NEAR_REACH, 0), S - win) for t0 in range(0, S, ATTN_TQ)]


def _attn_kernel(q_ref, k_ref, v_ref, qr_ref, kr_ref, vr_ref, nbias_ref, fbias_ref, o_ref,
                 of_scr, lf_scr):
    S = q_ref.shape[1]
    lane = lax.broadcasted_iota(jnp.int32, (1, LANES), 1)
    first_head = lane < A_HEAD_DIM

    def softmax_pv(q, k, v, bias):
        outs, lses = [], []
        for hh in range(LANES // A_HEAD_DIM):
            qh = jnp.where((lane // A_HEAD_DIM) == hh, q, jnp.zeros_like(q))
            s = jnp.einsum('rqd,rkd->rqk', qh, k, preferred_element_type=F32) + bias
            mx = jnp.max(s, axis=-1, keepdims=True)
            p = jnp.exp2(s - mx)
            den = jnp.sum(p, axis=-1, keepdims=True)
            outs.append(jnp.einsum('rqk,rkd->rqd', p.astype(BF16), v, preferred_element_type=F32) / den)
            lses.append(mx + jnp.log2(den))
        return jnp.where(first_head, outs[0], outs[1]), jnp.where(first_head, lses[0], lses[1])

    out_r, lse_r = softmax_pv(qr_ref[0], kr_ref[0], vr_ref[0], fbias_ref[...])
    for r in range(A_FAR_DIL):
        rows = pl.ds(r, S // A_FAR_DIL, stride=A_FAR_DIL)
        of_scr[rows, :] = out_r[r]
        lf_scr[rows, :] = lse_r[r]

    win, starts = _near_windows(S)
    windows = lambda ref: jnp.stack([ref[0, ws:ws + win, :] for ws in starts], axis=0)
    out_n, lse_n = softmax_pv(q_ref[0].reshape(len(starts), ATTN_TQ, LANES), windows(k_ref),
                              windows(v_ref), nbias_ref[...])
    out_n, lse_n = out_n.reshape(S, LANES), lse_n.reshape(S, LANES)

    out_f, lse_f = of_scr[...], lf_scr[...]
    mx = jnp.maximum(lse_n, lse_f)
    en, ef = jnp.exp2(lse_n - mx), jnp.exp2(lse_f - mx)
    o_ref[0] = ((en * out_n + ef * out_f) / (en + ef)).astype(o_ref.dtype)


def _near_bias_kernel(o_ref, *, starts):
    _, tq, win = o_ref.shape
    rel = (lax.broadcasted_iota(jnp.int32, (tq, win), 1)
           - lax.broadcasted_iota(jnp.int32, (tq, win), 0))
    for t, ws in enumerate(starts):
        d = rel + (ws - t * tq)
        dist = jnp.abs(d)
        count = jnp.zeros((tq, win), F32)
        for window, dil in A_PATTERNS[:-1]:
            hit = jnp.where((d & (dil - 1)) == 0, dist, 2 * A_NEAR_REACH + 1) <= window // 2
            count = count + jnp.where(hit, 1.0, 0.0)
        o_ref[t] = jnp.where(count > 0.5, jnp.log2(jnp.maximum(count, 1.0)), NEG_BIG)


def _far_bias_kernel(o_ref):
    n = o_ref.shape[0]
    d = lax.broadcasted_iota(jnp.int32, (n, n), 1) - lax.broadcasted_iota(jnp.int32, (n, n), 0)
    window, dil = A_PATTERNS[-1]
    o_ref[...] = jnp.where(jnp.abs(d) <= window // (2 * dil), 0.0, NEG_BIG)


def _attention_biases(S):
    win, starts = _near_windows(S)
    near = pl.pallas_call(
        functools.partial(_near_bias_kernel, starts=tuple(starts)),
        out_shape=jax.ShapeDtypeStruct((len(starts), ATTN_TQ, win), F32),
        name="near_bias",
    )()
    n = S // A_FAR_DIL
    far = pl.pallas_call(_far_bias_kernel, out_shape=jax.ShapeDtypeStruct((n, n), F32),
                         name="far_bias")()
    return near, far


def dilated_attention(qa, ka, va, qr, kr, vr, B, S):
    W = qa.shape[-1]
    q3, k3, v3 = (t.reshape(B, S, W) for t in (qa, ka, va))
    near, far = _attention_biases(S)
    seq = pl.BlockSpec((1, S, LANES), lambda b, p: (b, 0, p))
    res = pl.BlockSpec((1, A_FAR_DIL, S // A_FAR_DIL, LANES), lambda b, p: (b, 0, 0, p))
    out = pl.pallas_call(
        _attn_kernel,
        grid=(B, W // LANES),
        in_specs=[seq, seq, seq, res, res, res, _const_spec(near.shape), _const_spec(far.shape)],
        out_specs=seq,
        out_shape=jax.ShapeDtypeStruct((B, S, W), BF16),
        scratch_shapes=[pltpu.VMEM((S, LANES), F32)] * 2,
        compiler_params=_params("parallel", "parallel"),
        name="dilated_attention",
    )(q3, k3, v3, qr, kr, vr, near, far)
    return out.reshape(B * S, W)


def _log_sigmoid(x):
    return jnp.minimum(x, 0.0) - jnp.log1p(jnp.exp(-jnp.abs(x)))


def _split_bf16(x):
    hi = x.astype(BF16)
    return hi, (x - hi.astype(F32)).astype(BF16)


def _gla_kernel(q_ref, k_ref, v_ref, r_ref, z_ref, wgf_ref, wgb_ref, bgf_ref, bgb_ref, gn_ref,
                ltri_ref, utri_ref, o_ref, gf_scr, gb_scr, upd_scr, st_scr):
    S = q_ref.shape[1]
    C = B_CHUNK
    n_chunks = S // C
    kw, vw = q_ref.shape[2], v_ref.shape[2]
    ct = ltri_ref.shape[0]

    zb = z_ref[0].astype(BF16)
    for w_ref, b_ref, tri_ref, g_scr in ((wgf_ref, bgf_ref, ltri_ref, gf_scr),
                                         (wgb_ref, bgb_ref, utri_ref, gb_scr)):
        lg = _log_sigmoid(_dot(zb, w_ref[0]) + b_ref[0]) * (1.0 / B_TAU)
        hi_lo = jnp.concatenate(_split_bf16(lg), axis=1)
        for t in range(S // ct):
            rows = slice(t * ct, (t + 1) * ct)
            both = _dot(tri_ref[...], hi_lo[rows])
            g_scr[rows, :] = both[:, :kw] + both[:, kw:]

    row_v = lax.broadcasted_iota(jnp.int32, (vw, kw), 0)
    lane_k = lax.broadcasted_iota(jnp.int32, (vw, kw), 1)
    state_mask = (row_v // B_DV) == (lane_k // B_DK)
    lane1 = lax.broadcasted_iota(jnp.int32, (1, kw), 1)
    head_lane = [(lane1 // B_DK) == hh for hh in range(kw // B_DK)]
    col_v = lax.broadcasted_iota(jnp.int32, (1, vw), 1)
    head_col = [(col_v // B_DV) == hh for hh in range(vw // B_DV)]
    qi = lax.broadcasted_iota(jnp.int32, (C, kw), 0)
    kj = lax.broadcasted_iota(jnp.int32, (C, kw), 1) % C
    causal = kj <= qi

    state_mask2 = jnp.concatenate([state_mask, state_mask], axis=1)

    def chunk_rows(n):
        return pl.ds(pl.multiple_of(n * C, C), C)

    def increments(i, carry):
        for u in range(GLA_UNROLL):
            n = i * GLA_UNROLL + u
            rows = chunk_rows(n)
            gf, gb = gf_scr[rows, :], gb_scr[rows, :]
            k = k_ref[0, rows, :]
            kdec = jnp.concatenate([k * jnp.exp(gf[C - 1:C, :] - gf), k * jnp.exp(gb[0:1, :] - gb)], axis=1)
            upd_scr[n] = jnp.where(state_mask2, _dot_tn(v_ref[0, rows, :], kdec.astype(BF16)), 0.0)
        return carry

    lax.fori_loop(0, n_chunks // GLA_UNROLL, increments, 0)

    def recur(i, carry):
        sf, sb = carry
        nb = n_chunks - 1 - i
        st_scr[i, :, 0:kw] = sf.astype(BF16)
        st_scr[nb, :, kw:2 * kw] = sb.astype(BF16)
        af = jnp.exp(gf_scr[pl.ds(i * C + C - 1, 1), :])
        ab = jnp.exp(gb_scr[pl.ds(nb * C, 1), :])
        return af * sf + upd_scr[i, :, 0:kw], ab * sb + upd_scr[nb, :, kw:2 * kw]

    zero = jnp.zeros((vw, kw), F32)
    lax.fori_loop(0, n_chunks, recur, (zero, zero))

    U = GLA_BATCH

    def per_head_rows(t):
        return jnp.concatenate([jnp.where(m, t, jnp.zeros_like(t)) for m in head_lane], axis=1)

    def bdot_nt(a, b):
        return jnp.einsum('umk,unk->umn', a, b, preferred_element_type=F32)

    def outputs(i, carry):
        rows = pl.ds(pl.multiple_of(i * (U * C), U * C), U * C)
        chunks = lambda t: t.reshape(U, C, t.shape[-1])
        gf, gb = chunks(gf_scr[rows, :]), chunks(gb_scr[rows, :])
        q, k, v = chunks(q_ref[0, rows, :]), chunks(k_ref[0, rows, :]), chunks(v_ref[0, rows, :])
        qfb = jnp.concatenate([q * jnp.exp(gf), q * jnp.exp(gb)], axis=2).astype(BF16)
        kf, kb = (k * jnp.exp(-gf)).astype(BF16), (k * jnp.exp(-gb)).astype(BF16)
        att = jnp.where(causal, bdot_nt(qfb[:, :, :kw], per_head_rows(kf)),
                        bdot_nt(qfb[:, :, kw:], per_head_rows(kb)))
        v_heads = jnp.concatenate([jnp.where(m, v, jnp.zeros_like(v)) for m in head_col], axis=1)
        o = bdot_nt(qfb, st_scr[pl.ds(i * U, U)]) + jnp.einsum(
            'umk,ukn->umn', att.astype(BF16), v_heads, preferred_element_type=F32)
        normed = []
        for hh in range(vw // B_DV):
            oh = o[:, :, hh * B_DV:(hh + 1) * B_DV]
            normed.append(oh * lax.rsqrt(jnp.mean(oh * oh, axis=-1, keepdims=True) + EPS))
        gate = _silu(r_ref[0, rows, :].astype(F32))
        res = jnp.concatenate(normed, axis=2).reshape(U * C, vw) * gn_ref[0] * gate
        o_ref[0, rows, :] = res.astype(o_ref.dtype)
        return carry

    lax.fori_loop(0, n_chunks // U, outputs, 0)


def _chunk_tri(ct, chunk, upper):
    i = np.arange(ct)[:, None]
    j = np.arange(ct)[None, :]
    same = (i // chunk) == (j // chunk)
    return jnp.asarray(same & ((j >= i) if upper else (j <= i)), BF16)


def gla(qb, kb, vb, rb, z, wg2, bg, g_norm, B, S):
    npair = B_HEADS // 2
    kw, vw = 2 * B_DK, 2 * B_DV
    q3, k3 = qb.reshape(B, S, npair * kw), kb.reshape(B, S, npair * kw)
    v3, r3 = vb.reshape(B, S, npair * vw), rb.reshape(B, S, npair * vw)
    z3 = z.reshape(B, S, 2 * B_RANK)
    zero = jnp.zeros((B_RANK, B_HEADS * B_DK), F32)
    wgf = jnp.concatenate([wg2[0], zero], axis=0).astype(BF16)
    wgb = jnp.concatenate([zero, wg2[1]], axis=0).astype(BF16)
    pairs = lambda w: w.reshape(w.shape[0], npair, kw).transpose(1, 0, 2)
    ct = 256
    pair_spec = lambda shape: pl.BlockSpec((1,) + shape, lambda b, p: (p, 0, 0))
    seq_spec = lambda w: pl.BlockSpec((1, S, w), lambda b, p: (b, 0, p))
    out = pl.pallas_call(
        _gla_kernel,
        grid=(B, npair),
        in_specs=[seq_spec(kw), seq_spec(kw), seq_spec(vw), seq_spec(vw),
                  pl.BlockSpec((1, S, 2 * B_RANK), lambda b, p: (b, 0, 0)),
                  pair_spec((2 * B_RANK, kw)), pair_spec((2 * B_RANK, kw)),
                  pair_spec((1, kw)), pair_spec((1, kw)), pair_spec((1, vw)),
                  _const_spec((ct, ct)), _const_spec((ct, ct))],
        out_specs=seq_spec(vw),
        out_shape=jax.ShapeDtypeStruct((B, S, npair * vw), BF16),
        scratch_shapes=[pltpu.VMEM((S, kw), F32), pltpu.VMEM((S, kw), F32),
                        pltpu.VMEM((S // B_CHUNK, vw, 2 * kw), F32),
                        pltpu.VMEM((S // B_CHUNK, vw, 2 * kw), BF16)],
        compiler_params=_params("parallel", "parallel"),
        name="gla",
    )(q3, k3, v3, r3, z3, pairs(wgf), pairs(wgb), pairs(bg[0][None, :]), pairs(bg[1][None, :]),
      g_norm.reshape(npair, 1, vw), _chunk_tri(ct, B_CHUNK, False), _chunk_tri(ct, B_CHUNK, True))
    return out.reshape(B * S, npair * vw)


def _atom_transpose(x):
    rows, width = x.shape
    r = lax.broadcasted_iota(jnp.int32, (rows, width), 0)
    a = lax.broadcasted_iota(jnp.int32, (rows, width), 1) // C_GROUP
    for s in range(3):
        d = 1 << s
        rbit = (r & d) != 0
        abit = (a & d) != 0
        partner_row = jnp.where(rbit, pltpu.roll(x, d, axis=0), pltpu.roll(x, rows - d, axis=0))
        moved = jnp.where(abit, pltpu.roll(partner_row, C_GROUP * d, axis=1),
                          pltpu.roll(partner_row, width - C_GROUP * d, axis=1))
        x = jnp.where(rbit == abit, x, moved)
    return x


def _pack_groups(u, x_ref, z_scr):
    rows, width = u.shape
    n_chunks, B = x_ref.shape[1], x_ref.shape[2]
    R = rows // B
    z = _atom_transpose(u)
    for k in range(width // LANES):
        z_scr[k] = z[:, LANES * k:LANES * (k + 1)]
    for k in range(width // LANES):
        for g8 in range(SUBLANES):
            for c in range(n_chunks):
                lo = z_scr[k, pl.ds(c * S5_CHUNK + g8, B, stride=R), :]
                hi = z_scr[k, pl.ds(c * S5_CHUNK + SUBLANES + g8, B, stride=R), :]
                x_ref[SUBLANES * k + g8, c, :, :] = jnp.concatenate([lo, hi], axis=1).astype(x_ref.dtype)


def _unpack_groups(y_ref, z_scr):
    n_tiles, tm, _ = z_scr.shape
    nch = tm // S5_CHUNK
    for k in range(n_tiles):
        for g8 in range(SUBLANES):
            y = y_ref[SUBLANES * k + g8, 0, :, :]
            z_scr[k, pl.ds(g8, nch, stride=S5_CHUNK), :] = y[:, :LANES]
            z_scr[k, pl.ds(SUBLANES + g8, nch, stride=S5_CHUNK), :] = y[:, LANES:]
    return _atom_transpose(jnp.concatenate([z_scr[k] for k in range(n_tiles)], axis=1))


def _in_cd_kernel(x_ref, g_ref, wu_ref, wval_ref, wgate_ref, u_ref, gd_ref, xg_ref, z_scr):
    B, R, D = x_ref.shape
    xn = _rmsnorm(x_ref[...].reshape(B * R, D), g_ref[...]).astype(BF16)
    u = _dot(xn, wu_ref[...])
    u_ref[...] = u.reshape(u_ref.shape)
    _pack_groups(u, xg_ref, z_scr)
    gd = _dot(xn, wval_ref[...]) * _sigmoid(_dot(xn, wgate_ref[...]))
    gd_ref[...] = gd.reshape(gd_ref.shape)


def in_proj_cd(h, g, w_in, B, S):
    T, D = h.shape
    cw = C_GROUP * C_NGROUPS
    dw = (w_in.shape[1] - cw) // 2
    wb = w_in.astype(BF16)
    R = ROW_TILE // B
    xw = S5_CHUNK * C_GROUP
    slab = lambda w: pl.BlockSpec((B, R, w), lambda i: (0, i, 0))
    u, gd, xg = pl.pallas_call(
        _in_cd_kernel,
        grid=(S // R,),
        in_specs=[slab(D), _const_spec((1, D)), _const_spec((D, cw)), _const_spec((D, dw)),
                  _const_spec((D, dw))],
        out_specs=[slab(cw), slab(dw),
                   pl.BlockSpec((C_NGROUPS, R // S5_CHUNK, B, xw), lambda i: (0, i, 0, 0))],
        out_shape=[jax.ShapeDtypeStruct((B, S, cw), F32), jax.ShapeDtypeStruct((B, S, dw), F32),
                   jax.ShapeDtypeStruct((C_NGROUPS, S // S5_CHUNK, B, xw), BF16)],
        scratch_shapes=[pltpu.VMEM((cw // LANES, B * R, LANES), F32)],
        compiler_params=_params("parallel"),
        name="in_proj_cd",
    )(h.reshape(B, S, D), g[None, :], wb[:, :cw], wb[:, cw:cw + dw], wb[:, cw + dw:])
    return u.reshape(T, cw), gd.reshape(T, dw), xg


def _s5_operators(lam_re, lam_im, log_dt, b_re, b_im, c_re, c_im):
    hp = lax.Precision.HIGH
    L = S5_CHUNK
    tau = jnp.arange(L + 1, dtype=F32)
    ks, vs, ws, aL = [], [], [], []
    for d in range(2):
        lr, li = lam_re[d], lam_im[d]
        dt = jnp.exp(log_dt[d])[:, None]
        mag = jnp.exp(lr * dt)
        ab_re, ab_im = mag * jnp.cos(li * dt), mag * jnp.sin(li * dt)
        den = lr * lr + li * li
        nr = ab_re - 1.0
        f_re = (nr * lr + ab_im * li) / den
        f_im = (ab_im * lr - nr * li) / den
        bb_re = f_re[..., None] * b_re[d] - f_im[..., None] * b_im[d]
        bb_im = f_re[..., None] * b_im[d] + f_im[..., None] * b_re[d]
        pmag = jnp.exp(lr[None] * dt[None] * tau[:, None, None])
        pr = pmag * jnp.cos(li[None] * dt[None] * tau[:, None, None])
        pi = pmag * jnp.sin(li[None] * dt[None] * tau[:, None, None])
        ca_re = c_re[d][None] * pr[:, :, None, :] - c_im[d][None] * pi[:, :, None, :]
        ca_im = c_re[d][None] * pi[:, :, None, :] + c_im[d][None] * pr[:, :, None, :]
        k = (jnp.einsum('tghp,gpk->gkth', ca_re[:L], bb_re, precision=hp)
             - jnp.einsum('tghp,gpk->gkth', ca_im[:L], bb_im, precision=hp))
        ab_pow_re = pr[:, :, :, None] * bb_re[None] - pi[:, :, :, None] * bb_im[None]
        ab_pow_im = pr[:, :, :, None] * bb_im[None] + pi[:, :, :, None] * bb_re[None]
        order_v = (lambda t: t[L - 1::-1]) if d == 0 else (lambda t: t[:L])
        order_w = (lambda t: t[1:L + 1]) if d == 0 else (lambda t: t[L:0:-1])
        v = jnp.concatenate([order_v(ab_pow_re), order_v(ab_pow_im)], axis=2)
        vs.append(v.transpose(1, 0, 3, 2).reshape(v.shape[1], L * C_GROUP, 2 * C_STATE))
        w = jnp.concatenate([order_w(ca_re), -order_w(ca_im)], axis=3)
        ws.append(w.transpose(1, 3, 0, 2).reshape(w.shape[1], 2 * C_STATE, L * C_GROUP))
        ks.append(k)
        aL.append(jnp.stack([pr[L], pi[L]], axis=0))
    kf, kb = ks
    by_lag = jnp.concatenate([kb[:, :, :0:-1], kf[:, :, :1] + kb[:, :, :1], kf[:, :, 1:]], axis=2)
    n = by_lag.shape[0]
    m = jnp.stack([by_lag[:, :, L - 1 - j:2 * L - 1 - j].reshape(n, C_GROUP, L * C_GROUP)
                   for j in range(L)], axis=1)
    m = m.reshape(n, L * C_GROUP, L * C_GROUP)
    return m, vs[0], vs[1], ws[0], ws[1], jnp.stack(aL, axis=0)


def _pair_blockdiag(t):
    G, r, c = t.shape
    t = t.reshape(G // 2, 2, r, c)
    z = jnp.zeros((G // 2, r, c), t.dtype)
    return jnp.concatenate([jnp.concatenate([t[:, 0], z], axis=2),
                            jnp.concatenate([z, t[:, 1]], axis=2)], axis=1)


def _s5_pair_operators(params):
    fold = lambda t: jnp.moveaxis(t, 1, 0).reshape((2, t.shape[0] * t.shape[2]) + t.shape[3:])
    m, vf, vb, wf, wb, aL = _s5_operators(*(fold(t) for t in params))
    P = C_STATE
    mm = _pair_blockdiag(m)
    vcols = [_pair_blockdiag(v[:, :, s]) for v in (vf, vb) for s in (slice(0, P), slice(P, 2 * P))]
    vv = jnp.concatenate(vcols, axis=2)
    wrows = [_pair_blockdiag(w[:, s, :]) for w in (wf, wb) for s in (slice(0, P), slice(P, 2 * P))]
    ww = jnp.concatenate(wrows, axis=1)
    G = aL.shape[2]
    aa = aL.reshape(4, G // 2, 2 * P).transpose(1, 0, 2)
    return mm.astype(BF16), vv.astype(BF16), ww.astype(BF16), aa


def _s5_kernel(x_ref, m_ref, v_ref, w_ref, a_ref, y_ref, v_scr, s_scr):
    _, N, B, half = x_ref.shape
    lw = a_ref.shape[2]
    x = jnp.concatenate([x_ref[0].reshape(N * B, half), x_ref[1].reshape(N * B, half)], axis=1)
    v = _dot(x, v_ref[0])
    for p in range(4):
        v_scr[p] = v[:, lw * p:lw * (p + 1)]
    a = a_ref[0]
    afr, afi, abr, abi = a[0:1], a[1:2], a[2:3], a[3:4]

    def step(n, carry):
        fr, fi, br, bi = carry
        rf = pl.ds(pl.multiple_of(n * B, B), B)
        rb = pl.ds(pl.multiple_of((N - 1 - n) * B, B), B)
        s_scr[0, rf, :] = fr
        s_scr[1, rf, :] = fi
        s_scr[2, rb, :] = br
        s_scr[3, rb, :] = bi
        return (afr * fr - afi * fi + v_scr[0, rf, :], afr * fi + afi * fr + v_scr[1, rf, :],
                abr * br - abi * bi + v_scr[2, rb, :], abr * bi + abi * br + v_scr[3, rb, :])

    zero = jnp.zeros((B, lw), F32)
    lax.fori_loop(0, N, step, (zero, zero, zero, zero))
    s = jnp.concatenate([s_scr[p] for p in range(4)], axis=1).astype(BF16)
    y = _dot(x, m_ref[0]) + _dot(s, w_ref[0])
    parts = 2 * half // lw
    for p in range(parts):
        v_scr[p] = y[:, lw * p:lw * (p + 1)]
    for b in range(B):
        for p in range(parts):
            lanes = pl.ds((p * lw) % half, lw)
            y_ref[p * lw // half, b, :, lanes] = v_scr[p, pl.ds(b, N, stride=B), :]


def s5_bidirectional(xg, operators, layer):
    G, N, B, half = xg.shape
    mm, vv, ww, aa = operators
    pw = 2 * half
    sw = vv.shape[2]
    first = layer * (G // 2)
    op_spec = lambda shape: pl.BlockSpec((1,) + shape, lambda p: (first + p, 0, 0))
    pair_spec = lambda d1, d2: pl.BlockSpec((2, d1, d2, half), lambda p: (p, 0, 0, 0))
    return pl.pallas_call(
        _s5_kernel,
        grid=(G // 2,),
        in_specs=[pair_spec(N, B), op_spec((pw, pw)), op_spec((pw, sw)), op_spec((sw, pw)),
                  op_spec((4, sw // 4))],
        out_specs=pair_spec(B, N),
        out_shape=jax.ShapeDtypeStruct((G, B, N, half), F32),
        scratch_shapes=[pltpu.VMEM((4, B * N, sw // 4), F32)] * 2,
        compiler_params=_params("parallel"),
        name="s5",
    )(xg, mm, vv, ww, aa)


def _halo_rows(prev_ref, next_ref, tiles_per_seq):
    t = pl.program_id(0) % tiles_per_seq
    prev = jnp.where(t > 0, prev_ref[...], 0.0)
    nxt = jnp.where(t < tiles_per_seq - 1, next_ref[...], 0.0)
    return prev, nxt


def _cd_out_kernel(y_ref, u_ref, gd_ref, gdp_ref, gdn_ref, h_ref, dskip_ref, wglu_ref, bglu_ref,
                   cw_ref, cb_ref, lng_ref, lnb_ref, wc_ref, wd_ref, o_ref, ext_scr, z_scr, sh_scr, *,
                   tiles_per_seq):
    tm = u_ref.shape[0]
    z = jax.nn.gelu(_unpack_groups(y_ref, z_scr) + dskip_ref[...] * u_ref[...])
    o_c = z * _sigmoid(_dot(z.astype(BF16), wglu_ref[...]) + bglu_ref[...])
    prev, nxt = _halo_rows(gdp_ref, gdn_ref, tiles_per_seq)
    ext_scr[0:CONV_HALO, :] = prev
    ext_scr[CONV_HALO:CONV_HALO + tm, :] = gd_ref[...]
    ext_scr[CONV_HALO + tm:CONV_HALO + tm + CONV_HALO, :] = nxt
    n_rows = tm + 2 * CONV_HALO - SUBLANES
    for r in range(1, SUBLANES):
        sh_scr[r - 1] = ext_scr[r:r + n_rows, :]
    base = CONV_HALO - (D_KERNEL - 1) // 2
    acc = jnp.zeros(gd_ref.shape, F32) + cb_ref[...]
    for kk in range(D_KERNEL):
        q, r = divmod(base + kk, SUBLANES)
        rows = slice(q * SUBLANES, q * SUBLANES + tm)
        tap = ext_scr[rows, :] if r == 0 else sh_scr[r - 1, rows, :]
        acc = acc + cw_ref[kk:kk + 1, :] * tap
    mu = jnp.mean(acc, axis=-1, keepdims=True)
    cen = acc - mu
    var = jnp.mean(cen * cen, axis=-1, keepdims=True)
    o_d = _silu(cen * lax.rsqrt(var + EPS) * lng_ref[...] + lnb_ref[...])
    o_ref[...] = (h_ref[...] + _dot(o_c.astype(BF16), wc_ref[...])
                  + _dot(o_d.astype(BF16), wd_ref[...]))


def cd_out(yg, u, gd, h, d_skip, w_glu, b_glu, conv_w, conv_b, ln_g, ln_b, w_out, S):
    T, D = h.shape
    cw, dw = u.shape[1], gd.shape[1]
    tm = ROW_TILE
    tiles_per_seq = tps = S // tm
    hb = tm // CONV_HALO
    n_halo_blocks = T // CONV_HALO
    row = lambda v: v[None, :]
    return pl.pallas_call(
        functools.partial(_cd_out_kernel, tiles_per_seq=tiles_per_seq),
        grid=(T // tm,),
        in_specs=[pl.BlockSpec((yg.shape[0], 1, tm // S5_CHUNK, yg.shape[3]),
                               lambda i: (0, i // tps, i % tps, 0)),
                  _row_spec(tm, cw), _row_spec(tm, dw),
                  pl.BlockSpec((CONV_HALO, dw), lambda i: (jnp.maximum(i * hb - 1, 0), 0)),
                  pl.BlockSpec((CONV_HALO, dw), lambda i: (jnp.minimum((i + 1) * hb, n_halo_blocks - 1), 0)),
                  _row_spec(tm, D), _const_spec((1, cw)), _const_spec((cw, cw)), _const_spec((1, cw)),
                  _const_spec((D_KERNEL, dw)), _const_spec((1, dw)), _const_spec((1, dw)),
                  _const_spec((1, dw)), _const_spec((cw, D)), _const_spec((dw, D))],
        out_specs=_row_spec(tm, D),
        out_shape=jax.ShapeDtypeStruct((T, D), F32),
        scratch_shapes=[pltpu.VMEM((tm + 2 * CONV_HALO, dw), F32), pltpu.VMEM((cw // LANES, tm, LANES), F32),
                        pltpu.VMEM((SUBLANES - 1, tm + 2 * CONV_HALO - SUBLANES, dw), F32)],
        compiler_params=_params("parallel"),
        name="cd_out",
    )(yg, u, gd, gd, gd, h, row(d_skip), w_glu.astype(BF16), row(b_glu), conv_w, row(conv_b),
      row(ln_g), row(ln_b), w_out[:cw].astype(BF16), w_out[cw:].astype(BF16))


def _kv_kernel(m_ref, g_ref, w_ref, o_ref):
    o_ref[...] = _dot(_rmsnorm(m_ref[...], g_ref[...]).astype(BF16), w_ref[0]).astype(o_ref.dtype)


def mem_kv(mem2, g, wkv, layer):
    T, D = mem2.shape
    tm = ROW_TILE
    width = wkv.shape[2]
    return pl.pallas_call(
        _kv_kernel,
        grid=(T // tm,),
        in_specs=[_row_spec(tm, D), _const_spec((1, D)), _layer_spec((D, width), layer)],
        out_specs=_row_spec(tm, width),
        out_shape=jax.ShapeDtypeStruct((T, width), BF16),
        compiler_params=_params("parallel"),
        name="mem_kv",
    )(mem2, g[None, :], wkv)


def _xattn_kernel(x_ref, g_ref, wq_ref, k_ref, v_ref, wo_ref, *rest, mixer_out):
    if mixer_out:
        a_ref, b_ref, wa_ref, wb_ref, o_ref = rest
        x = x_ref[...] + _dot(a_ref[...], wa_ref[...]) + _dot(b_ref[...], wb_ref[...])
    else:
        (o_ref,) = rest
        x = x_ref[...]
    hd = x.shape[1] // X_HEADS
    q = (_dot(_rmsnorm(x, g_ref[...]).astype(BF16), wq_ref[0]) * (hd ** -0.5)).astype(BF16)
    heads = []
    for hh in range(X_HEADS):
        cols = slice(hh * hd, (hh + 1) * hd)
        s = _dot_nt(q[:, cols], k_ref[0, :, cols])
        p = jnp.exp(s - jnp.max(s, axis=-1, keepdims=True))
        den = jnp.sum(p, axis=-1, keepdims=True)
        heads.append((_dot(p.astype(BF16), v_ref[0, :, cols]) / den).astype(BF16))
    o_ref[...] = x + _dot(jnp.concatenate(heads, axis=1), wo_ref[0])


def cross_attention(h, g, wq, kv, wo, B, S, layer, mixer=None):
    T, D = h.shape
    M = kv.shape[0] // B
    kv3 = kv.reshape(B, M, 2 * D)
    tm = ROW_TILE
    tps = S // tm
    in_specs = [_row_spec(tm, D), _const_spec((1, D)), _layer_spec((D, D), layer),
                pl.BlockSpec((1, M, D), lambda i: (i // tps, 0, 0)),
                pl.BlockSpec((1, M, D), lambda i: (i // tps, 0, 1)),
                _layer_spec((D, D), layer)]
    args = [h, g[None, :], wq, kv3, kv3, wo]
    if mixer is not None:
        a, b, w_out = mixer
        wa, wb = w_out[:a.shape[1]].astype(BF16), w_out[a.shape[1]:].astype(BF16)
        in_specs += [_row_spec(tm, a.shape[1]), _row_spec(tm, b.shape[1]), _const_spec(wa.shape),
                     _const_spec(wb.shape)]
        args += [a, b, wa, wb]
    return pl.pallas_call(
        functools.partial(_xattn_kernel, mixer_out=mixer is not None),
        grid=(T // tm,),
        in_specs=in_specs,
        out_specs=_row_spec(tm, D),
        out_shape=jax.ShapeDtypeStruct((T, D), F32),
        compiler_params=_params("parallel"),
        name="cross_attention",
    )(*args)


def _ffn_kernel(x_ref, xp_ref, xn_ref, g_ref, wv_ref, wg_ref, cwv_ref, cwg_ref, cbv_ref, cbg_ref,
                wd_ref, gfin_ref, o_ref, act_scr, *, tiles_per_seq, n_split, final_norm):
    tm = x_ref.shape[0]
    x = x_ref[...]
    prev, nxt = _halo_rows(xp_ref, xn_ref, tiles_per_seq)
    xe = _rmsnorm(jnp.concatenate([prev, x, nxt], axis=0), g_ref[...]).astype(BF16)
    rows_ext = xe.shape[0]
    fw = wv_ref.shape[2] // n_split
    mid = slice(FFN_HALO, FFN_HALO + tm)

    def conv(u, cw_ref, cb_ref, cols):
        before = pltpu.roll(u, 1, axis=0)[mid]
        after = pltpu.roll(u, rows_ext - 1, axis=0)[mid]
        return (cb_ref[:, cols] + cw_ref[0:1, cols] * before + cw_ref[1:2, cols] * u[mid]
                + cw_ref[2:3, cols] * after)

    for c in range(n_split):
        cols = slice(c * fw, (c + 1) * fw)
        gate = conv(_dot(xe, wg_ref[0, :, cols]), cwg_ref, cbg_ref, cols)
        val = conv(_dot(xe, wv_ref[0, :, cols]), cwv_ref, cbv_ref, cols)
        act_scr[:, cols] = (_silu(gate) * val).astype(BF16)
    acc = x + _dot(act_scr[...], wd_ref[0])
    if final_norm:
        acc = _rmsnorm(acc, gfin_ref[...])
    o_ref[...] = acc


def conv_ffn(h, g, w_up, w_conv, b_conv, w_down, g_final, S, layer, final_norm):
    T, D = h.shape
    F = w_down.shape[1]
    tm = ROW_TILE
    tiles_per_seq = S // tm
    hb = tm // FFN_HALO
    n_halo_blocks = T // FFN_HALO
    n_split = F // FFN_COLS
    row = lambda v: v[None, :]
    return pl.pallas_call(
        functools.partial(_ffn_kernel, tiles_per_seq=tiles_per_seq, n_split=n_split,
                          final_norm=final_norm),
        grid=(T // tm,),
        in_specs=[_row_spec(tm, D),
                  pl.BlockSpec((FFN_HALO, D), lambda i: (jnp.maximum(i * hb - 1, 0), 0)),
                  pl.BlockSpec((FFN_HALO, D), lambda i: (jnp.minimum((i + 1) * hb, n_halo_blocks - 1), 0)),
                  _const_spec((1, D)), _layer_spec((D, F), layer, 0), _layer_spec((D, F), layer, 1),
                  _const_spec((FFN_KERNEL, F)), _const_spec((FFN_KERNEL, F)),
                  _const_spec((1, F)), _const_spec((1, F)), _layer_spec((F, D), layer),
                  _const_spec((1, D))],
        out_specs=_row_spec(tm, D),
        out_shape=jax.ShapeDtypeStruct((T, D), F32),
        scratch_shapes=[pltpu.VMEM((tm, F), BF16)],
        compiler_params=_params("parallel"),
        name="conv_ffn",
    )(h, h, h, row(g), w_up, w_up, w_conv[:, :F], w_conv[:, F:], row(b_conv[:F]),
      row(b_conv[F:]), w_down, row(g_final))


def kernel(x, mem, positions, g_mix, g_xattn, g_mem, w_xq, w_xkv, w_xo, g_ffn, w_up, w_conv_ffn,
           b_conv_ffn, w_down, w_in_ab, w_out_ab, gla_wg2, gla_bg, gla_norm, w_in_cd, w_out_cd,
           s5_lam_re, s5_lam_im, s5_log_dt, s5_b_re, s5_b_im, s5_c_re, s5_c_im, s5_d, s5_w_glu,
           s5_b_glu, conv_w, conv_b, conv_ln_g, conv_ln_b, g_final):
    B, S, D = x.shape
    depth = g_mix.shape[0]
    assert S % ROW_TILE == 0 and S % B_CHUNK == 0 and S % S5_CHUNK == 0
    h = x.reshape(B * S, D)
    mem2 = mem.reshape(-1, D)
    cos, sin = rope_tables(positions)
    s5_ops = _s5_pair_operators((s5_lam_re, s5_lam_im, s5_log_dt, s5_b_re, s5_b_im, s5_c_re, s5_c_im))
    w_xq, w_xkv, w_xo, w_up, w_down = (w.astype(BF16) for w in (w_xq, w_xkv, w_xo, w_up, w_down))
    for layer in range(depth):
        i = layer // 2
        if layer % 2 == 0:
            qa, ka, va, qb, kb, vb, rb, z, qr, kr, vr = in_proj_ab(h, g_mix[layer], cos, sin,
                                                                   w_in_ab[i], B, S)
            o_a = dilated_attention(qa, ka, va, qr, kr, vr, B, S)
            o_b = gla(qb, kb, vb, rb, z, gla_wg2[i], gla_bg[i], gla_norm[i], B, S)
            mixer = (o_a, o_b, w_out_ab[i])
        else:
            mixer = None
            u, gd, xg = in_proj_cd(h, g_mix[layer], w_in_cd[i], B, S)
            yg = s5_bidirectional(xg, s5_ops, i)
            h = cd_out(yg, u, gd, h, s5_d[i], s5_w_glu[i], s5_b_glu[i], conv_w[i], conv_b[i],
                       conv_ln_g[i], conv_ln_b[i], w_out_cd[i], S)
        kv = mem_kv(mem2, g_mem[layer], w_xkv, layer)
        h = cross_attention(h, g_xattn[layer], w_xq, kv, w_xo, B, S, layer, mixer)
        h = conv_ffn(h, g_ffn[layer], w_up, w_conv_ffn[layer], b_conv_ffn[layer], w_down, g_final,
                     S, layer, final_norm=(layer == depth - 1))
    return h.reshape(B, S, D)
```

```python
import functools
import math

import jax
import jax.numpy as jnp
import numpy as np
from jax import lax
from jax.experimental import pallas as pl
from jax.experimental.pallas import tpu as pltpu

F32 = jnp.float32
BF16 = jnp.bfloat16
EPS = 1e-6

LANES = 128
SUBLANES = 8
VMEM_LIMIT_BYTES = 52 * 1024 * 1024

A_HEADS, A_HEAD_DIM = 8, 64
A_PATTERNS = ((128, 1), (512, 4), (2048, 16))
A_FAR_DIL = A_PATTERNS[-1][1]
A_NEAR_REACH = max(w // 2 for w, _ in A_PATTERNS[:-1])
ATTN_TQ = 256
ROPE_THETA = 10000.0
B_HEADS, B_DK, B_DV = 4, 64, 128
B_RANK, B_TAU, B_CHUNK = 16, 16.0, 64
C_GROUP, C_NGROUPS, C_STATE = 16, 32, 64
S5_CHUNK = 16
D_KERNEL = 31
X_HEADS = 4
FFN_KERNEL = 3
NEG_BIG = -1e30
LOG2E = math.log2(math.e)

GLA_UNROLL = 4
GLA_BATCH = 8
ROW_TILE = 512
WIDE_ROW_TILE = 1024
CONV_HALO = 16
FFN_HALO = 8
FFN_COLS = 256


def _params(*sem):
    return pltpu.CompilerParams(dimension_semantics=sem, vmem_limit_bytes=VMEM_LIMIT_BYTES)


def _const_spec(shape):
    zeros = (0,) * len(shape)
    return pl.BlockSpec(shape, lambda *_: zeros, pipeline_mode=pl.Buffered(1))


def _layer_spec(shape, layer, col_block=0):
    return pl.BlockSpec((1,) + shape, lambda *_: (layer, 0, col_block), pipeline_mode=pl.Buffered(1))


def _row_spec(tm, width):
    return pl.BlockSpec((tm, width), lambda i: (i, 0))


def _dot(a, b):
    return jnp.dot(a, b, preferred_element_type=F32)


def _dot_nt(a, b):
    return lax.dot_general(a, b, (((1,), (1,)), ((), ())), preferred_element_type=F32)


def _dot_tn(a, b):
    return lax.dot_general(a, b, (((0,), (0,)), ((), ())), preferred_element_type=F32)


def _rmsnorm(x, g):
    return x * lax.rsqrt(jnp.mean(x * x, axis=-1, keepdims=True) + EPS) * g


def _sigmoid(x):
    return 1.0 / (1.0 + jnp.exp(-x))


def _silu(x):
    return x * _sigmoid(x)


def _rope_table_kernel(pos_ref, invf_ref, cos_ref, sin_ref):
    ang = pos_ref[...].astype(F32) * invf_ref[...]
    lane = lax.broadcasted_iota(jnp.int32, (1, LANES), 1)
    sign = jnp.where((lane % A_HEAD_DIM) < A_HEAD_DIM // 2, -1.0, 1.0)
    cos_ref[...] = jnp.cos(ang)
    sin_ref[...] = jnp.sin(ang) * sign


def rope_tables(positions):
    T = positions.size
    tm = 1024
    inv_freq = ROPE_THETA ** (-jnp.arange(0, A_HEAD_DIM, 2, dtype=F32) / A_HEAD_DIM)
    invf = jnp.tile(inv_freq, LANES // (A_HEAD_DIM // 2))[None, :]
    return pl.pallas_call(
        _rope_table_kernel,
        grid=(T // tm,),
        in_specs=[_row_spec(tm, 1), _const_spec((1, LANES))],
        out_specs=[_row_spec(tm, LANES), _row_spec(tm, LANES)],
        out_shape=[jax.ShapeDtypeStruct((T, LANES), F32)] * 2,
        compiler_params=_params("parallel"),
        name="rope_tables",
    )(positions.reshape(T, 1), invf)


def _residue_permutation(tm):
    t = np.arange(tm)
    p = np.zeros((tm, tm), np.float32)
    p[(t % A_FAR_DIL) * (tm // A_FAR_DIL) + t // A_FAR_DIL, t] = 1.0
    return jnp.asarray(p, BF16)


def _in_ab_kernel(x_ref, g_ref, cos_ref, sin_ref, wqk_ref, wva_ref, wqkb_ref, wvr_ref, wz_ref, perm_ref,
                  qa_ref, ka_ref, va_ref, qb_ref, kb_ref, vb_ref, rb_ref, z_ref,
                  qr_ref, kr_ref, vr_ref):
    xn = _rmsnorm(x_ref[...], g_ref[...]).astype(BF16)
    qk = _dot(xn, wqk_ref[...])
    width = qk.shape[1]
    reps = width // LANES
    cos = jnp.concatenate([cos_ref[...]] * reps, axis=1)
    sin = jnp.concatenate([sin_ref[...]] * reps, axis=1)
    half = A_HEAD_DIM // 2
    lane = lax.broadcasted_iota(jnp.int32, (1, width), 1)
    first_half = (lane % A_HEAD_DIM) < half
    partner = jnp.where(first_half, pltpu.roll(qk, width - half, axis=1), pltpu.roll(qk, half, axis=1))
    roped = qk * cos + partner * sin
    aw = width // 2
    qa = roped[:, :aw] * (A_HEAD_DIM ** -0.5 * LOG2E)
    ka = roped[:, aw:]
    va = _dot(xn, wva_ref[...])
    qkv = jnp.concatenate([qa.astype(BF16), ka.astype(BF16), va.astype(BF16)], axis=1)
    qa_ref[...] = qkv[:, :aw]
    ka_ref[...] = qkv[:, aw:2 * aw]
    va_ref[...] = qkv[:, 2 * aw:]
    regrouped = _dot(perm_ref[...], qkv).astype(BF16)
    per_class = qkv.shape[0] // A_FAR_DIL
    for r in range(A_FAR_DIL):
        rows = slice(r * per_class, (r + 1) * per_class)
        qr_ref[0, r] = regrouped[rows, :aw]
        kr_ref[0, r] = regrouped[rows, aw:2 * aw]
        vr_ref[0, r] = regrouped[rows, 2 * aw:]
    qkb = _dot(xn, wqkb_ref[...])
    kw = qkb.shape[1] // 2
    qb_ref[...] = qkb[:, :kw] * (B_DK ** -0.5)
    kb_ref[...] = qkb[:, kw:]
    vr = _dot(xn, wvr_ref[...])
    vw = vr.shape[1] // 2
    vb_ref[...] = vr[:, :vw].astype(BF16)
    rb_ref[...] = vr[:, vw:].astype(BF16)
    z_ref[...] = _dot(xn, wz_ref[...])


def in_proj_ab(h, g, cos, sin, w_in, B, S):
    T, D = h.shape
    aw, kw, vw = A_HEADS * A_HEAD_DIM, B_HEADS * B_DK, B_HEADS * B_DV
    o = np.cumsum([0, aw, aw, aw, kw, kw, vw, vw, 2 * B_RANK])
    wb = w_in.astype(BF16)
    wqk, wva, wqkb, wvr, wz = (wb[:, o[0]:o[2]], wb[:, o[2]:o[3]], wb[:, o[3]:o[5]],
                               wb[:, o[5]:o[7]], wb[:, o[7]:o[8]])
    tm = ROW_TILE
    tps = S // tm
    outs = [(aw, BF16), (aw, BF16), (aw, BF16), (kw, F32), (kw, F32), (vw, BF16), (vw, BF16),
            (2 * B_RANK, F32)]
    res_spec = pl.BlockSpec((1, A_FAR_DIL, tm // A_FAR_DIL, aw), lambda i: (i // tps, 0, i % tps, 0))
    res_shape = jax.ShapeDtypeStruct((B, A_FAR_DIL, S // A_FAR_DIL, aw), BF16)
    return pl.pallas_call(
        _in_ab_kernel,
        grid=(T // tm,),
        in_specs=[_row_spec(tm, D), _const_spec((1, D)), _row_spec(tm, LANES), _row_spec(tm, LANES),
                  _const_spec(wqk.shape), _const_spec(wva.shape), _const_spec(wqkb.shape),
                  _const_spec(wvr.shape), _const_spec(wz.shape), _const_spec((tm, tm))],
        out_specs=[_row_spec(tm, w) for w, _ in outs] + [res_spec] * 3,
        out_shape=[jax.ShapeDtypeStruct((T, w), dt) for w, dt in outs] + [res_shape] * 3,
        compiler_params=_params("parallel"),
        name="in_proj_ab",
    )(h, g[None, :], cos, sin, wqk, wva, wqkb, wvr, wz, _residue_permutation(tm))


def _near_windows(S):
    win = ATTN_TQ + 2 * A_NEAR_REACH
    return win, [min(max(t0 - A_NEAR_REACH, 0), S - win) for t0 in range(0, S, ATTN_TQ)]


def _attn_kernel(q_ref, k_ref, v_ref, qr_ref, kr_ref, vr_ref, nbias_ref, fbias_ref, o_ref,
                 of_scr, lf_scr):
    S = q_ref.shape[1]
    lane = lax.broadcasted_iota(jnp.int32, (1, LANES), 1)
    first_head = lane < A_HEAD_DIM

    def softmax_pv(q, k, v, bias):
        outs, lses = [], []
        for hh in range(LANES // A_HEAD_DIM):
            qh = jnp.where((lane // A_HEAD_DIM) == hh, q, jnp.zeros_like(q))
            s = jnp.einsum('rqd,rkd->rqk', qh, k, preferred_element_type=F32) + bias
            mx = jnp.max(s, axis=-1, keepdims=True)
            p = jnp.exp2(s - mx)
            den = jnp.sum(p, axis=-1, keepdims=True)
            outs.append(jnp.einsum('rqk,rkd->rqd', p.astype(BF16), v, preferred_element_type=F32) / den)
            lses.append(mx + jnp.log2(den))
        return jnp.where(first_head, outs[0], outs[1]), jnp.where(first_head, lses[0], lses[1])

    out_r, lse_r = softmax_pv(qr_ref[0], kr_ref[0], vr_ref[0], fbias_ref[...])
    for r in range(A_FAR_DIL):
        rows = pl.ds(r, S // A_FAR_DIL, stride=A_FAR_DIL)
        of_scr[rows, :] = out_r[r]
        lf_scr[rows, :] = lse_r[r]

    win, starts = _near_windows(S)
    windows = lambda ref: jnp.stack([ref[0, ws:ws + win, :] for ws in starts], axis=0)
    out_n, lse_n = softmax_pv(q_ref[0].reshape(len(starts), ATTN_TQ, LANES), windows(k_ref),
                              windows(v_ref), nbias_ref[...])
    out_n, lse_n = out_n.reshape(S, LANES), lse_n.reshape(S, LANES)

    out_f, lse_f = of_scr[...], lf_scr[...]
    mx = jnp.maximum(lse_n, lse_f)
    en, ef = jnp.exp2(lse_n - mx), jnp.exp2(lse_f - mx)
    o_ref[0] = ((en * out_n + ef * out_f) / (en + ef)).astype(o_ref.dtype)


def _near_bias_kernel(o_ref, *, starts):
    _, tq, win = o_ref.shape
    rel = (lax.broadcasted_iota(jnp.int32, (tq, win), 1)
           - lax.broadcasted_iota(jnp.int32, (tq, win), 0))
    for t, ws in enumerate(starts):
        d = rel + (ws - t * tq)
        dist = jnp.abs(d)
        count = jnp.zeros((tq, win), F32)
        for window, dil in A_PATTERNS[:-1]:
            hit = jnp.where((d & (dil - 1)) == 0, dist, 2 * A_NEAR_REACH + 1) <= window // 2
            count = count + jnp.where(hit, 1.0, 0.0)
        o_ref[t] = jnp.where(count > 0.5, jnp.log2(jnp.maximum(count, 1.0)), NEG_BIG)


def _far_bias_kernel(o_ref):
    n = o_ref.shape[0]
    d = lax.broadcasted_iota(jnp.int32, (n, n), 1) - lax.broadcasted_iota(jnp.int32, (n, n), 0)
    window, dil = A_PATTERNS[-1]
    o_ref[...] = jnp.where(jnp.abs(d) <= window // (2 * dil), 0.0, NEG_BIG)


def _attention_biases(S):
    win, starts = _near_windows(S)
    near = pl.pallas_call(
        functools.partial(_near_bias_kernel, starts=tuple(starts)),
        out_shape=jax.ShapeDtypeStruct((len(starts), ATTN_TQ, win), F32),
        name="near_bias",
    )()
    n = S // A_FAR_DIL
    far = pl.pallas_call(_far_bias_kernel, out_shape=jax.ShapeDtypeStruct((n, n), F32),
                         name="far_bias")()
    return near, far


def dilated_attention(qa, ka, va, qr, kr, vr, B, S):
    W = qa.shape[-1]
    q3, k3, v3 = (t.reshape(B, S, W) for t in (qa, ka, va))
    near, far = _attention_biases(S)
    seq = pl.BlockSpec((1, S, LANES), lambda b, p: (b, 0, p))
    res = pl.BlockSpec((1, A_FAR_DIL, S // A_FAR_DIL, LANES), lambda b, p: (b, 0, 0, p))
    out = pl.pallas_call(
        _attn_kernel,
        grid=(B, W // LANES),
        in_specs=[seq, seq, seq, res, res, res, _const_spec(near.shape), _const_spec(far.shape)],
        out_specs=seq,
        out_shape=jax.ShapeDtypeStruct((B, S, W), BF16),
        scratch_shapes=[pltpu.VMEM((S, LANES), F32)] * 2,
        compiler_params=_params("parallel", "parallel"),
        name="dilated_attention",
    )(q3, k3, v3, qr, kr, vr, near, far)
    return out.reshape(B * S, W)


def _log_sigmoid(x):
    return jnp.minimum(x, 0.0) - jnp.log1p(jnp.exp(-jnp.abs(x)))


def _split_bf16(x):
    hi = x.astype(BF16)
    return hi, (x - hi.astype(F32)).astype(BF16)


def _gla_kernel(q_ref, k_ref, v_ref, r_ref, z_ref, wgf_ref, wgb_ref, bgf_ref, bgb_ref, gn_ref,
                ltri_ref, utri_ref, o_ref, gf_scr, gb_scr, upd_scr, st_scr):
    S = q_ref.shape[1]
    C = B_CHUNK
    n_chunks = S // C
    kw, vw = q_ref.shape[2], v_ref.shape[2]
    ct = ltri_ref.shape[0]

    zb = z_ref[0].astype(BF16)
    for w_ref, b_ref, tri_ref, g_scr in ((wgf_ref, bgf_ref, ltri_ref, gf_scr),
                                         (wgb_ref, bgb_ref, utri_ref, gb_scr)):
        lg = _log_sigmoid(_dot(zb, w_ref[0]) + b_ref[0]) * (1.0 / B_TAU)
        hi_lo = jnp.concatenate(_split_bf16(lg), axis=1)
        for t in range(S // ct):
            rows = slice(t * ct, (t + 1) * ct)
            both = _dot(tri_ref[...], hi_lo[rows])
            g_scr[rows, :] = both[:, :kw] + both[:, kw:]

    row_v = lax.broadcasted_iota(jnp.int32, (vw, kw), 0)
    lane_k = lax.broadcasted_iota(jnp.int32, (vw, kw), 1)
    state_mask = (row_v // B_DV) == (lane_k // B_DK)
    lane1 = lax.broadcasted_iota(jnp.int32, (1, kw), 1)
    head_lane = [(lane1 // B_DK) == hh for hh in range(kw // B_DK)]
    col_v = lax.broadcasted_iota(jnp.int32, (1, vw), 1)
    head_col = [(col_v // B_DV) == hh for hh in range(vw // B_DV)]
    qi = lax.broadcasted_iota(jnp.int32, (C, kw), 0)
    kj = lax.broadcasted_iota(jnp.int32, (C, kw), 1) % C
    causal = kj <= qi

    state_mask2 = jnp.concatenate([state_mask, state_mask], axis=1)

    def chunk_rows(n):
        return pl.ds(pl.multiple_of(n * C, C), C)

    def increments(i, carry):
        nu = GLA_UNROLL
        rows = pl.ds(pl.multiple_of(i * (nu * C), nu * C), nu * C)
        chunks = lambda t: t.reshape(nu, C, t.shape[-1])
        gf, gb, k = chunks(gf_scr[rows, :]), chunks(gb_scr[rows, :]), chunks(k_ref[0, rows, :])
        kdec = jnp.concatenate([k * jnp.exp(gf[:, C - 1:C, :] - gf), k * jnp.exp(gb[:, 0:1, :] - gb)],
                               axis=2).astype(BF16)
        upd = jnp.einsum('ucv,uck->uvk', chunks(v_ref[0, rows, :]), kdec, preferred_element_type=F32)
        upd_scr[pl.ds(i * nu, nu)] = jnp.where(state_mask2, upd, 0.0)
        return carry

    lax.fori_loop(0, n_chunks // GLA_UNROLL, increments, 0)

    def recur(i, carry):
        sf, sb = carry
        nb = n_chunks - 1 - i
        st_scr[i, :, 0:kw] = sf.astype(BF16)
        st_scr[nb, :, kw:2 * kw] = sb.astype(BF16)
        af = jnp.exp(gf_scr[pl.ds(i * C + C - 1, 1), :])
        ab = jnp.exp(gb_scr[pl.ds(nb * C, 1), :])
        return af * sf + upd_scr[i, :, 0:kw], ab * sb + upd_scr[nb, :, kw:2 * kw]

    zero = jnp.zeros((vw, kw), F32)
    lax.fori_loop(0, n_chunks, recur, (zero, zero))

    U = GLA_BATCH

    def per_head_rows(t):
        return jnp.concatenate([jnp.where(m, t, jnp.zeros_like(t)) for m in head_lane], axis=1)

    def bdot_nt(a, b):
        return jnp.einsum('umk,unk->umn', a, b, preferred_element_type=F32)

    def outputs(i, carry):
        rows = pl.ds(pl.multiple_of(i * (U * C), U * C), U * C)
        chunks = lambda t: t.reshape(U, C, t.shape[-1])
        gf, gb = chunks(gf_scr[rows, :]), chunks(gb_scr[rows, :])
        q, k, v = chunks(q_ref[0, rows, :]), chunks(k_ref[0, rows, :]), chunks(v_ref[0, rows, :])
        qfb = jnp.concatenate([q * jnp.exp(gf), q * jnp.exp(gb)], axis=2).astype(BF16)
        kf, kb = (k * jnp.exp(-gf)).astype(BF16), (k * jnp.exp(-gb)).astype(BF16)
        att = jnp.where(causal, bdot_nt(qfb[:, :, :kw], per_head_rows(kf)),
                        bdot_nt(qfb[:, :, kw:], per_head_rows(kb)))
        v_heads = jnp.concatenate([jnp.where(m, v, jnp.zeros_like(v)) for m in head_col], axis=1)
        o = bdot_nt(qfb, st_scr[pl.ds(i * U, U)]) + jnp.einsum(
            'umk,ukn->umn', att.astype(BF16), v_heads, preferred_element_type=F32)
        normed = []
        for hh in range(vw // B_DV):
            oh = o[:, :, hh * B_DV:(hh + 1) * B_DV]
            normed.append(oh * lax.rsqrt(jnp.mean(oh * oh, axis=-1, keepdims=True) + EPS))
        gate = _silu(r_ref[0, rows, :].astype(F32))
        res = jnp.concatenate(normed, axis=2).reshape(U * C, vw) * gn_ref[0] * gate
        o_ref[0, rows, :] = res.astype(o_ref.dtype)
        return carry

    lax.fori_loop(0, n_chunks // U, outputs, 0)


def _chunk_tri(ct, chunk, upper):
    i = np.arange(ct)[:, None]
    j = np.arange(ct)[None, :]
    same = (i // chunk) == (j // chunk)
    return jnp.asarray(same & ((j >= i) if upper else (j <= i)), BF16)


def gla(qb, kb, vb, rb, z, wg2, bg, g_norm, B, S):
    npair = B_HEADS // 2
    kw, vw = 2 * B_DK, 2 * B_DV
    q3, k3 = qb.reshape(B, S, npair * kw), kb.reshape(B, S, npair * kw)
    v3, r3 = vb.reshape(B, S, npair * vw), rb.reshape(B, S, npair * vw)
    z3 = z.reshape(B, S, 2 * B_RANK)
    zero = jnp.zeros((B_RANK, B_HEADS * B_DK), F32)
    wgf = jnp.concatenate([wg2[0], zero], axis=0).astype(BF16)
    wgb = jnp.concatenate([zero, wg2[1]], axis=0).astype(BF16)
    pairs = lambda w: w.reshape(w.shape[0], npair, kw).transpose(1, 0, 2)
    ct = 256
    pair_spec = lambda shape: pl.BlockSpec((1,) + shape, lambda b, p: (p, 0, 0))
    seq_spec = lambda w: pl.BlockSpec((1, S, w), lambda b, p: (b, 0, p))
    out = pl.pallas_call(
        _gla_kernel,
        grid=(B, npair),
        in_specs=[seq_spec(kw), seq_spec(kw), seq_spec(vw), seq_spec(vw),
                  pl.BlockSpec((1, S, 2 * B_RANK), lambda b, p: (b, 0, 0)),
                  pair_spec((2 * B_RANK, kw)), pair_spec((2 * B_RANK, kw)),
                  pair_spec((1, kw)), pair_spec((1, kw)), pair_spec((1, vw)),
                  _const_spec((ct, ct)), _const_spec((ct, ct))],
        out_specs=seq_spec(vw),
        out_shape=jax.ShapeDtypeStruct((B, S, npair * vw), BF16),
        scratch_shapes=[pltpu.VMEM((S, kw), F32), pltpu.VMEM((S, kw), F32),
                        pltpu.VMEM((S // B_CHUNK, vw, 2 * kw), F32),
                        pltpu.VMEM((S // B_CHUNK, vw, 2 * kw), BF16)],
        compiler_params=_params("parallel", "parallel"),
        name="gla",
    )(q3, k3, v3, r3, z3, pairs(wgf), pairs(wgb), pairs(bg[0][None, :]), pairs(bg[1][None, :]),
      g_norm.reshape(npair, 1, vw), _chunk_tri(ct, B_CHUNK, False), _chunk_tri(ct, B_CHUNK, True))
    return out.reshape(B * S, npair * vw)


def _atom_transpose(x):
    rows, width = x.shape
    r = lax.broadcasted_iota(jnp.int32, (rows, width), 0)
    a = lax.broadcasted_iota(jnp.int32, (rows, width), 1) // C_GROUP
    for s in range(3):
        d = 1 << s
        rbit = (r & d) != 0
        abit = (a & d) != 0
        partner_row = jnp.where(rbit, pltpu.roll(x, d, axis=0), pltpu.roll(x, rows - d, axis=0))
        moved = jnp.where(abit, pltpu.roll(partner_row, C_GROUP * d, axis=1),
                          pltpu.roll(partner_row, width - C_GROUP * d, axis=1))
        x = jnp.where(rbit == abit, x, moved)
    return x


def _pack_groups(u, x_ref, z_scr):
    rows, width = u.shape
    n_chunks, B = x_ref.shape[1], x_ref.shape[2]
    R = rows // B
    z = _atom_transpose(u)
    for k in range(width // LANES):
        z_scr[k] = z[:, LANES * k:LANES * (k + 1)]
    for k in range(width // LANES):
        for g8 in range(SUBLANES):
            for c in range(n_chunks):
                lo = z_scr[k, pl.ds(c * S5_CHUNK + g8, B, stride=R), :]
                hi = z_scr[k, pl.ds(c * S5_CHUNK + SUBLANES + g8, B, stride=R), :]
                x_ref[SUBLANES * k + g8, c, :, :] = jnp.concatenate([lo, hi], axis=1).astype(x_ref.dtype)


def _unpack_groups(y_ref, z_scr):
    n_tiles, tm, _ = z_scr.shape
    nch = tm // S5_CHUNK
    for k in range(n_tiles):
        for g8 in range(SUBLANES):
            y = y_ref[SUBLANES * k + g8, 0, :, :]
            z_scr[k, pl.ds(g8, nch, stride=S5_CHUNK), :] = y[:, :LANES]
            z_scr[k, pl.ds(SUBLANES + g8, nch, stride=S5_CHUNK), :] = y[:, LANES:]
    return _atom_transpose(jnp.concatenate([z_scr[k] for k in range(n_tiles)], axis=1))


def _in_cd_kernel(x_ref, g_ref, wu_ref, wval_ref, wgate_ref, u_ref, gd_ref, xg_ref, z_scr):
    B, R, D = x_ref.shape
    xn = _rmsnorm(x_ref[...].reshape(B * R, D), g_ref[...]).astype(BF16)
    u = _dot(xn, wu_ref[...])
    u_ref[...] = u.reshape(u_ref.shape)
    _pack_groups(u, xg_ref, z_scr)
    gd = _dot(xn, wval_ref[...]) * _sigmoid(_dot(xn, wgate_ref[...]))
    gd_ref[...] = gd.reshape(gd_ref.shape)


def in_proj_cd(h, g, w_in, B, S):
    T, D = h.shape
    cw = C_GROUP * C_NGROUPS
    dw = (w_in.shape[1] - cw) // 2
    wb = w_in.astype(BF16)
    R = ROW_TILE // B
    xw = S5_CHUNK * C_GROUP
    slab = lambda w: pl.BlockSpec((B, R, w), lambda i: (0, i, 0))
    u, gd, xg = pl.pallas_call(
        _in_cd_kernel,
        grid=(S // R,),
        in_specs=[slab(D), _const_spec((1, D)), _const_spec((D, cw)), _const_spec((D, dw)),
                  _const_spec((D, dw))],
        out_specs=[slab(cw), slab(dw),
                   pl.BlockSpec((C_NGROUPS, R // S5_CHUNK, B, xw), lambda i: (0, i, 0, 0))],
        out_shape=[jax.ShapeDtypeStruct((B, S, cw), F32), jax.ShapeDtypeStruct((B, S, dw), F32),
                   jax.ShapeDtypeStruct((C_NGROUPS, S // S5_CHUNK, B, xw), BF16)],
        scratch_shapes=[pltpu.VMEM((cw // LANES, B * R, LANES), F32)],
        compiler_params=_params("parallel"),
        name="in_proj_cd",
    )(h.reshape(B, S, D), g[None, :], wb[:, :cw], wb[:, cw:cw + dw], wb[:, cw + dw:])
    return u.reshape(T, cw), gd.reshape(T, dw), xg


def _s5_operators(lam_re, lam_im, log_dt, b_re, b_im, c_re, c_im):
    hp = lax.Precision.HIGH
    L = S5_CHUNK
    tau = jnp.arange(L + 1, dtype=F32)
    ks, vs, ws, aL = [], [], [], []
    for d in range(2):
        lr, li = lam_re[d], lam_im[d]
        dt = jnp.exp(log_dt[d])[:, None]
        mag = jnp.exp(lr * dt)
        ab_re, ab_im = mag * jnp.cos(li * dt), mag * jnp.sin(li * dt)
        den = lr * lr + li * li
        nr = ab_re - 1.0
        f_re = (nr * lr + ab_im * li) / den
        f_im = (ab_im * lr - nr * li) / den
        bb_re = f_re[..., None] * b_re[d] - f_im[..., None] * b_im[d]
        bb_im = f_re[..., None] * b_im[d] + f_im[..., None] * b_re[d]
        pmag = jnp.exp(lr[None] * dt[None] * tau[:, None, None])
        pr = pmag * jnp.cos(li[None] * dt[None] * tau[:, None, None])
        pi = pmag * jnp.sin(li[None] * dt[None] * tau[:, None, None])
        ca_re = c_re[d][None] * pr[:, :, None, :] - c_im[d][None] * pi[:, :, None, :]
        ca_im = c_re[d][None] * pi[:, :, None, :] + c_im[d][None] * pr[:, :, None, :]
        k = (jnp.einsum('tghp,gpk->gkth', ca_re[:L], bb_re, precision=hp)
             - jnp.einsum('tghp,gpk->gkth', ca_im[:L], bb_im, precision=hp))
        ab_pow_re = pr[:, :, :, None] * bb_re[None] - pi[:, :, :, None] * bb_im[None]
        ab_pow_im = pr[:, :, :, None] * bb_im[None] + pi[:, :, :, None] * bb_re[None]
        order_v = (lambda t: t[L - 1::-1]) if d == 0 else (lambda t: t[:L])
        order_w = (lambda t: t[1:L + 1]) if d == 0 else (lambda t: t[L:0:-1])
        v = jnp.concatenate([order_v(ab_pow_re), order_v(ab_pow_im)], axis=2)
        vs.append(v.transpose(1, 0, 3, 2).reshape(v.shape[1], L * C_GROUP, 2 * C_STATE))
        w = jnp.concatenate([order_w(ca_re), -order_w(ca_im)], axis=3)
        ws.append(w.transpose(1, 3, 0, 2).reshape(w.shape[1], 2 * C_STATE, L * C_GROUP))
        ks.append(k)
        aL.append(jnp.stack([pr[L], pi[L]], axis=0))
    kf, kb = ks
    by_lag = jnp.concatenate([kb[:, :, :0:-1], kf[:, :, :1] + kb[:, :, :1], kf[:, :, 1:]], axis=2)
    n = by_lag.shape[0]
    m = jnp.stack([by_lag[:, :, L - 1 - j:2 * L - 1 - j].reshape(n, C_GROUP, L * C_GROUP)
                   for j in range(L)], axis=1)
    m = m.reshape(n, L * C_GROUP, L * C_GROUP)
    return m, vs[0], vs[1], ws[0], ws[1], jnp.stack(aL, axis=0)


def _pair_blockdiag(t):
    G, r, c = t.shape
    t = t.reshape(G // 2, 2, r, c)
    z = jnp.zeros((G // 2, r, c), t.dtype)
    return jnp.concatenate([jnp.concatenate([t[:, 0], z], axis=2),
                            jnp.concatenate([z, t[:, 1]], axis=2)], axis=1)


def _s5_pair_operators(params):
    fold = lambda t: jnp.moveaxis(t, 1, 0).reshape((2, t.shape[0] * t.shape[2]) + t.shape[3:])
    m, vf, vb, wf, wb, aL = _s5_operators(*(fold(t) for t in params))
    P = C_STATE
    mm = _pair_blockdiag(m)
    vcols = [_pair_blockdiag(v[:, :, s]) for v in (vf, vb) for s in (slice(0, P), slice(P, 2 * P))]
    vv = jnp.concatenate(vcols, axis=2)
    wrows = [_pair_blockdiag(w[:, s, :]) for w in (wf, wb) for s in (slice(0, P), slice(P, 2 * P))]
    ww = jnp.concatenate(wrows, axis=1)
    G = aL.shape[2]
    aa = aL.reshape(4, G // 2, 2 * P).transpose(1, 0, 2)
    return mm.astype(BF16), vv.astype(BF16), ww.astype(BF16), aa


def _s5_kernel(x_ref, m_ref, v_ref, w_ref, a_ref, y_ref, v_scr, s_scr):
    _, N, B, half = x_ref.shape
    lw = a_ref.shape[2]
    x = jnp.concatenate([x_ref[0].reshape(N * B, half), x_ref[1].reshape(N * B, half)], axis=1)
    v = _dot(x, v_ref[0])
    for p in range(4):
        v_scr[p] = v[:, lw * p:lw * (p + 1)]
    a = a_ref[0]
    afr, afi, abr, abi = a[0:1], a[1:2], a[2:3], a[3:4]

    def step(n, carry):
        fr, fi, br, bi = carry
        rf = pl.ds(pl.multiple_of(n * B, B), B)
        rb = pl.ds(pl.multiple_of((N - 1 - n) * B, B), B)
        s_scr[0, rf, :] = fr
        s_scr[1, rf, :] = fi
        s_scr[2, rb, :] = br
        s_scr[3, rb, :] = bi
        return (afr * fr - afi * fi + v_scr[0, rf, :], afr * fi + afi * fr + v_scr[1, rf, :],
                abr * br - abi * bi + v_scr[2, rb, :], abr * bi + abi * br + v_scr[3, rb, :])

    zero = jnp.zeros((B, lw), F32)
    lax.fori_loop(0, N, step, (zero, zero, zero, zero))
    s = jnp.concatenate([s_scr[p] for p in range(4)], axis=1).astype(BF16)
    y = _dot(x, m_ref[0]) + _dot(s, w_ref[0])
    parts = 2 * half // lw
    for p in range(parts):
        v_scr[p] = y[:, lw * p:lw * (p + 1)]
    for b in range(B):
        for p in range(parts):
            lanes = pl.ds((p * lw) % half, lw)
            y_ref[p * lw // half, b, :, lanes] = v_scr[p, pl.ds(b, N, stride=B), :]


def s5_bidirectional(xg, operators, layer):
    G, N, B, half = xg.shape
    mm, vv, ww, aa = operators
    pw = 2 * half
    sw = vv.shape[2]
    first = layer * (G // 2)
    op_spec = lambda shape: pl.BlockSpec((1,) + shape, lambda p: (first + p, 0, 0))
    pair_spec = lambda d1, d2: pl.BlockSpec((2, d1, d2, half), lambda p: (p, 0, 0, 0))
    return pl.pallas_call(
        _s5_kernel,
        grid=(G // 2,),
        in_specs=[pair_spec(N, B), op_spec((pw, pw)), op_spec((pw, sw)), op_spec((sw, pw)),
                  op_spec((4, sw // 4))],
        out_specs=pair_spec(B, N),
        out_shape=jax.ShapeDtypeStruct((G, B, N, half), F32),
        scratch_shapes=[pltpu.VMEM((4, B * N, sw // 4), F32)] * 2,
        compiler_params=_params("parallel"),
        name="s5",
    )(xg, mm, vv, ww, aa)


def _halo_rows(prev_ref, next_ref, tiles_per_seq):
    t = pl.program_id(0) % tiles_per_seq
    prev = jnp.where(t > 0, prev_ref[...], 0.0)
    nxt = jnp.where(t < tiles_per_seq - 1, next_ref[...], 0.0)
    return prev, nxt


def _cd_out_kernel(y_ref, u_ref, gd_ref, gdp_ref, gdn_ref, h_ref, dskip_ref, wglu_ref, bglu_ref,
                   cw_ref, cb_ref, lng_ref, lnb_ref, wc_ref, wd_ref, o_ref, ext_scr, z_scr, sh_scr, *,
                   tiles_per_seq):
    tm = u_ref.shape[0]
    z = jax.nn.gelu(_unpack_groups(y_ref, z_scr) + dskip_ref[...] * u_ref[...])
    o_c = z * _sigmoid(_dot(z.astype(BF16), wglu_ref[...]) + bglu_ref[...])
    prev, nxt = _halo_rows(gdp_ref, gdn_ref, tiles_per_seq)
    ext_scr[0:CONV_HALO, :] = prev
    ext_scr[CONV_HALO:CONV_HALO + tm, :] = gd_ref[...]
    ext_scr[CONV_HALO + tm:CONV_HALO + tm + CONV_HALO, :] = nxt
    n_rows = tm + 2 * CONV_HALO - SUBLANES
    for r in range(1, SUBLANES):
        sh_scr[r - 1] = ext_scr[r:r + n_rows, :]
    base = CONV_HALO - (D_KERNEL - 1) // 2
    acc = jnp.zeros(gd_ref.shape, F32) + cb_ref[...]
    for kk in range(D_KERNEL):
        q, r = divmod(base + kk, SUBLANES)
        rows = slice(q * SUBLANES, q * SUBLANES + tm)
        tap = ext_scr[rows, :] if r == 0 else sh_scr[r - 1, rows, :]
        acc = acc + cw_ref[kk:kk + 1, :] * tap
    mu = jnp.mean(acc, axis=-1, keepdims=True)
    cen = acc - mu
    var = jnp.mean(cen * cen, axis=-1, keepdims=True)
    o_d = _silu(cen * lax.rsqrt(var + EPS) * lng_ref[...] + lnb_ref[...])
    o_ref[...] = (h_ref[...] + _dot(o_c.astype(BF16), wc_ref[...])
                  + _dot(o_d.astype(BF16), wd_ref[...]))


def cd_out(yg, u, gd, h, d_skip, w_glu, b_glu, conv_w, conv_b, ln_g, ln_b, w_out, S):
    T, D = h.shape
    cw, dw = u.shape[1], gd.shape[1]
    tm = ROW_TILE
    tiles_per_seq = tps = S // tm
    hb = tm // CONV_HALO
    n_halo_blocks = T // CONV_HALO
    row = lambda v: v[None, :]
    return pl.pallas_call(
        functools.partial(_cd_out_kernel, tiles_per_seq=tiles_per_seq),
        grid=(T // tm,),
        in_specs=[pl.BlockSpec((yg.shape[0], 1, tm // S5_CHUNK, yg.shape[3]),
                               lambda i: (0, i // tps, i % tps, 0)),
                  _row_spec(tm, cw), _row_spec(tm, dw),
                  pl.BlockSpec((CONV_HALO, dw), lambda i: (jnp.maximum(i * hb - 1, 0), 0)),
                  pl.BlockSpec((CONV_HALO, dw), lambda i: (jnp.minimum((i + 1) * hb, n_halo_blocks - 1), 0)),
                  _row_spec(tm, D), _const_spec((1, cw)), _const_spec((cw, cw)), _const_spec((1, cw)),
                  _const_spec((D_KERNEL, dw)), _const_spec((1, dw)), _const_spec((1, dw)),
                  _const_spec((1, dw)), _const_spec((cw, D)), _const_spec((dw, D))],
        out_specs=_row_spec(tm, D),
        out_shape=jax.ShapeDtypeStruct((T, D), F32),
        scratch_shapes=[pltpu.VMEM((tm + 2 * CONV_HALO, dw), F32), pltpu.VMEM((cw // LANES, tm, LANES), F32),
                        pltpu.VMEM((SUBLANES - 1, tm + 2 * CONV_HALO - SUBLANES, dw), F32)],
        compiler_params=_params("parallel"),
        name="cd_out",
    )(yg, u, gd, gd, gd, h, row(d_skip), w_glu.astype(BF16), row(b_glu), conv_w, row(conv_b),
      row(ln_g), row(ln_b), w_out[:cw].astype(BF16), w_out[cw:].astype(BF16))


def _kv_kernel(m_ref, g_ref, w_ref, o_ref):
    o_ref[...] = _dot(_rmsnorm(m_ref[...], g_ref[...]).astype(BF16), w_ref[0]).astype(o_ref.dtype)


def mem_kv(mem2, g, wkv, layer):
    T, D = mem2.shape
    tm = ROW_TILE
    width = wkv.shape[2]
    return pl.pallas_call(
        _kv_kernel,
        grid=(T // tm,),
        in_specs=[_row_spec(tm, D), _const_spec((1, D)), _layer_spec((D, width), layer)],
        out_specs=_row_spec(tm, width),
        out_shape=jax.ShapeDtypeStruct((T, width), BF16),
        compiler_params=_params("parallel"),
        name="mem_kv",
    )(mem2, g[None, :], wkv)


def _xattn_kernel(x_ref, g_ref, wq_ref, k_ref, v_ref, wo_ref, *rest, mixer_out):
    if mixer_out:
        a_ref, b_ref, wa_ref, wb_ref, o_ref = rest
        x = x_ref[...] + _dot(a_ref[...], wa_ref[...]) + _dot(b_ref[...], wb_ref[...])
    else:
        (o_ref,) = rest
        x = x_ref[...]
    hd = x.shape[1] // X_HEADS
    q = (_dot(_rmsnorm(x, g_ref[...]).astype(BF16), wq_ref[0]) * (hd ** -0.5)).astype(BF16)
    heads = []
    for hh in range(X_HEADS):
        cols = slice(hh * hd, (hh + 1) * hd)
        s = _dot_nt(q[:, cols], k_ref[0, :, cols])
        p = jnp.exp(s - jnp.max(s, axis=-1, keepdims=True))
        den = jnp.sum(p, axis=-1, keepdims=True)
        heads.append((_dot(p.astype(BF16), v_ref[0, :, cols]) / den).astype(BF16))
    o_ref[...] = x + _dot(jnp.concatenate(heads, axis=1), wo_ref[0])


def cross_attention(h, g, wq, kv, wo, B, S, layer, mixer=None):
    T, D = h.shape
    M = kv.shape[0] // B
    kv3 = kv.reshape(B, M, 2 * D)
    tm = WIDE_ROW_TILE
    tps = S // tm
    in_specs = [_row_spec(tm, D), _const_spec((1, D)), _layer_spec((D, D), layer),
                pl.BlockSpec((1, M, D), lambda i: (i // tps, 0, 0)),
                pl.BlockSpec((1, M, D), lambda i: (i // tps, 0, 1)),
                _layer_spec((D, D), layer)]
    args = [h, g[None, :], wq, kv3, kv3, wo]
    if mixer is not None:
        a, b, w_out = mixer
        wa, wb = w_out[:a.shape[1]].astype(BF16), w_out[a.shape[1]:].astype(BF16)
        in_specs += [_row_spec(tm, a.shape[1]), _row_spec(tm, b.shape[1]), _const_spec(wa.shape),
                     _const_spec(wb.shape)]
        args += [a, b, wa, wb]
    return pl.pallas_call(
        functools.partial(_xattn_kernel, mixer_out=mixer is not None),
        grid=(T // tm,),
        in_specs=in_specs,
        out_specs=_row_spec(tm, D),
        out_shape=jax.ShapeDtypeStruct((T, D), F32),
        compiler_params=_params("parallel"),
        name="cross_attention",
    )(*args)


def _ffn_kernel(x_ref, xp_ref, xn_ref, g_ref, wv_ref, wg_ref, cwv_ref, cwg_ref, cbv_ref, cbg_ref,
                wd_ref, gfin_ref, o_ref, act_scr, *, tiles_per_seq, n_split, final_norm):
    tm = x_ref.shape[0]
    x = x_ref[...]
    prev, nxt = _halo_rows(xp_ref, xn_ref, tiles_per_seq)
    xe = _rmsnorm(jnp.concatenate([prev, x, nxt], axis=0), g_ref[...]).astype(BF16)
    rows_ext = xe.shape[0]
    fw = wv_ref.shape[2] // n_split
    mid = slice(FFN_HALO, FFN_HALO + tm)

    def conv(u, cw_ref, cb_ref, cols):
        before = pltpu.roll(u, 1, axis=0)[mid]
        after = pltpu.roll(u, rows_ext - 1, axis=0)[mid]
        return (cb_ref[:, cols] + cw_ref[0:1, cols] * before + cw_ref[1:2, cols] * u[mid]
                + cw_ref[2:3, cols] * after)

    for c in range(n_split):
        cols = slice(c * fw, (c + 1) * fw)
        gate = conv(_dot(xe, wg_ref[0, :, cols]), cwg_ref, cbg_ref, cols)
        val = conv(_dot(xe, wv_ref[0, :, cols]), cwv_ref, cbv_ref, cols)
        act_scr[:, cols] = (_silu(gate) * val).astype(BF16)
    acc = x + _dot(act_scr[...], wd_ref[0])
    if final_norm:
        acc = _rmsnorm(acc, gfin_ref[...])
    o_ref[...] = acc


def conv_ffn(h, g, w_up, w_conv, b_conv, w_down, g_final, S, layer, final_norm):
    T, D = h.shape
    F = w_down.shape[1]
    tm = ROW_TILE
    tiles_per_seq = S // tm
    hb = tm // FFN_HALO
    n_halo_blocks = T // FFN_HALO
    n_split = F // FFN_COLS
    row = lambda v: v[None, :]
    return pl.pallas_call(
        functools.partial(_ffn_kernel, tiles_per_seq=tiles_per_seq, n_split=n_split,
                          final_norm=final_norm),
        grid=(T // tm,),
        in_specs=[_row_spec(tm, D),
                  pl.BlockSpec((FFN_HALO, D), lambda i: (jnp.maximum(i * hb - 1, 0), 0)),
                  pl.BlockSpec((FFN_HALO, D), lambda i: (jnp.minimum((i + 1) * hb, n_halo_blocks - 1), 0)),
                  _const_spec((1, D)), _layer_spec((D, F), layer, 0), _layer_spec((D, F), layer, 1),
                  _const_spec((FFN_KERNEL, F)), _const_spec((FFN_KERNEL, F)),
                  _const_spec((1, F)), _const_spec((1, F)), _layer_spec((F, D), layer),
                  _const_spec((1, D))],
        out_specs=_row_spec(tm, D),
        out_shape=jax.ShapeDtypeStruct((T, D), F32),
        scratch_shapes=[pltpu.VMEM((tm, F), BF16)],
        compiler_params=_params("parallel"),
        name="conv_ffn",
    )(h, h, h, row(g), w_up, w_up, w_conv[:, :F], w_conv[:, F:], row(b_conv[:F]),
      row(b_conv[F:]), w_down, row(g_final))


def kernel(x, mem, positions, g_mix, g_xattn, g_mem, w_xq, w_xkv, w_xo, g_ffn, w_up, w_conv_ffn,
           b_conv_ffn, w_down, w_in_ab, w_out_ab, gla_wg2, gla_bg, gla_norm, w_in_cd, w_out_cd,
           s5_lam_re, s5_lam_im, s5_log_dt, s5_b_re, s5_b_im, s5_c_re, s5_c_im, s5_d, s5_w_glu,
           s5_b_glu, conv_w, conv_b, conv_ln_g, conv_ln_b, g_final):
    B, S, D = x.shape
    depth = g_mix.shape[0]
    assert S % ROW_TILE == 0 and S % B_CHUNK == 0 and S % S5_CHUNK == 0
    h = x.reshape(B * S, D)
    mem2 = mem.reshape(-1, D)
    cos, sin = rope_tables(positions)
    s5_ops = _s5_pair_operators((s5_lam_re, s5_lam_im, s5_log_dt, s5_b_re, s5_b_im, s5_c_re, s5_c_im))
    w_xq, w_xkv, w_xo, w_up, w_down = (w.astype(BF16) for w in (w_xq, w_xkv, w_xo, w_up, w_down))
    for layer in range(depth):
        i = layer // 2
        if layer % 2 == 0:
            qa, ka, va, qb, kb, vb, rb, z, qr, kr, vr = in_proj_ab(h, g_mix[layer], cos, sin,
                                                                   w_in_ab[i], B, S)
            o_a = dilated_attention(qa, ka, va, qr, kr, vr, B, S)
            o_b = gla(qb, kb, vb, rb, z, gla_wg2[i], gla_bg[i], gla_norm[i], B, S)
            mixer = (o_a, o_b, w_out_ab[i])
        else:
            mixer = None
            u, gd, xg = in_proj_cd(h, g_mix[layer], w_in_cd[i], B, S)
            yg = s5_bidirectional(xg, s5_ops, i)
            h = cd_out(yg, u, gd, h, s5_d[i], s5_w_glu[i], s5_b_glu[i], conv_w[i], conv_b[i],
                       conv_ln_g[i], conv_ln_b[i], w_out_cd[i], S)
        kv = mem_kv(mem2, g_mem[layer], w_xkv, layer)
        h = cross_attention(h, g_xattn[layer], w_xq, kv, w_xo, B, S, layer, mixer)
        h = conv_ffn(h, g_ffn[layer], w_up, w_conv_ffn[layer], b_conv_ffn[layer], w_down, g_final,
                     S, layer, final_norm=(layer == depth - 1))
    return h.reshape(B, S, D)
```

```python
import functools
import math

import jax
import jax.numpy as jnp
import numpy as np
from jax import lax
from jax.experimental import pallas as pl
from jax.experimental.pallas import tpu as pltpu

F32 = jnp.float32
BF16 = jnp.bfloat16
EPS = 1e-6

LANES = 128
SUBLANES = 8
VMEM_LIMIT_BYTES = 52 * 1024 * 1024

A_HEADS, A_HEAD_DIM = 8, 64
A_PATTERNS = ((128, 1), (512, 4), (2048, 16))
A_FAR_DIL = A_PATTERNS[-1][1]
A_NEAR_REACH = max(w // 2 for w, _ in A_PATTERNS[:-1])
ATTN_TQ = 256
ROPE_THETA = 10000.0
B_HEADS, B_DK, B_DV = 4, 64, 128
B_RANK, B_TAU, B_CHUNK = 16, 16.0, 64
C_GROUP, C_NGROUPS, C_STATE = 16, 32, 64
S5_CHUNK = 16
D_KERNEL = 31
X_HEADS = 4
FFN_KERNEL = 3
NEG_BIG = -1e30
LOG2E = math.log2(math.e)

GLA_UNROLL = 4
GLA_BATCH = 8
ROW_TILE = 512
WIDE_ROW_TILE = 1024
CONV_HALO = 16
FFN_HALO = 8
FFN_COLS = 256


def _params(*sem):
    return pltpu.CompilerParams(dimension_semantics=sem, vmem_limit_bytes=VMEM_LIMIT_BYTES)


def _const_spec(shape):
    zeros = (0,) * len(shape)
    return pl.BlockSpec(shape, lambda *_: zeros, pipeline_mode=pl.Buffered(1))


def _layer_spec(shape, layer, col_block=0):
    return pl.BlockSpec((1,) + shape, lambda *_: (layer, 0, col_block), pipeline_mode=pl.Buffered(1))


def _row_spec(tm, width):
    return pl.BlockSpec((tm, width), lambda i: (i, 0))


def _dot(a, b):
    return jnp.dot(a, b, preferred_element_type=F32)


def _dot_nt(a, b):
    return lax.dot_general(a, b, (((1,), (1,)), ((), ())), preferred_element_type=F32)


def _dot_tn(a, b):
    return lax.dot_general(a, b, (((0,), (0,)), ((), ())), preferred_element_type=F32)


def _rmsnorm(x, g):
    return x * lax.rsqrt(jnp.mean(x * x, axis=-1, keepdims=True) + EPS) * g


def _sigmoid(x):
    return 1.0 / (1.0 + jnp.exp(-x))


def _silu(x):
    return x * _sigmoid(x)


def _rope_table_kernel(pos_ref, invf_ref, cos_ref, sin_ref):
    ang = pos_ref[...].astype(F32) * invf_ref[...]
    lane = lax.broadcasted_iota(jnp.int32, (1, LANES), 1)
    sign = jnp.where((lane % A_HEAD_DIM) < A_HEAD_DIM // 2, -1.0, 1.0)
    cos_ref[...] = jnp.cos(ang)
    sin_ref[...] = jnp.sin(ang) * sign


def rope_tables(positions):
    T = positions.size
    tm = 1024
    inv_freq = ROPE_THETA ** (-jnp.arange(0, A_HEAD_DIM, 2, dtype=F32) / A_HEAD_DIM)
    invf = jnp.tile(inv_freq, LANES // (A_HEAD_DIM // 2))[None, :]
    return pl.pallas_call(
        _rope_table_kernel,
        grid=(T // tm,),
        in_specs=[_row_spec(tm, 1), _const_spec((1, LANES))],
        out_specs=[_row_spec(tm, LANES), _row_spec(tm, LANES)],
        out_shape=[jax.ShapeDtypeStruct((T, LANES), F32)] * 2,
        compiler_params=_params("parallel"),
        name="rope_tables",
    )(positions.reshape(T, 1), invf)


def _residue_permutation(tm):
    t = np.arange(tm)
    p = np.zeros((tm, tm), np.float32)
    p[(t % A_FAR_DIL) * (tm // A_FAR_DIL) + t // A_FAR_DIL, t] = 1.0
    return jnp.asarray(p, BF16)


def _in_ab_kernel(x_ref, g_ref, cos_ref, sin_ref, wqk_ref, wva_ref, wqkb_ref, wvr_ref, wz_ref, perm_ref,
                  qa_ref, ka_ref, va_ref, qb_ref, kb_ref, vb_ref, rb_ref, z_ref,
                  qr_ref, kr_ref, vr_ref):
    xn = _rmsnorm(x_ref[...], g_ref[...]).astype(BF16)
    qk = _dot(xn, wqk_ref[...])
    width = qk.shape[1]
    reps = width // LANES
    cos = jnp.concatenate([cos_ref[...]] * reps, axis=1)
    sin = jnp.concatenate([sin_ref[...]] * reps, axis=1)
    half = A_HEAD_DIM // 2
    lane = lax.broadcasted_iota(jnp.int32, (1, width), 1)
    first_half = (lane % A_HEAD_DIM) < half
    partner = jnp.where(first_half, pltpu.roll(qk, width - half, axis=1), pltpu.roll(qk, half, axis=1))
    roped = qk * cos + partner * sin
    aw = width // 2
    qa = roped[:, :aw] * (A_HEAD_DIM ** -0.5 * LOG2E)
    ka = roped[:, aw:]
    va = _dot(xn, wva_ref[...])
    qkv = jnp.concatenate([qa.astype(BF16), ka.astype(BF16), va.astype(BF16)], axis=1)
    qa_ref[...] = qkv[:, :aw]
    ka_ref[...] = qkv[:, aw:2 * aw]
    va_ref[...] = qkv[:, 2 * aw:]
    regrouped = _dot(perm_ref[...], qkv).astype(BF16)
    per_class = qkv.shape[0] // A_FAR_DIL
    for r in range(A_FAR_DIL):
        rows = slice(r * per_class, (r + 1) * per_class)
        qr_ref[0, r] = regrouped[rows, :aw]
        kr_ref[0, r] = regrouped[rows, aw:2 * aw]
        vr_ref[0, r] = regrouped[rows, 2 * aw:]
    qkb = _dot(xn, wqkb_ref[...])
    kw = qkb.shape[1] // 2
    qb_ref[...] = qkb[:, :kw] * (B_DK ** -0.5)
    kb_ref[...] = qkb[:, kw:]
    vr = _dot(xn, wvr_ref[...])
    vw = vr.shape[1] // 2
    vb_ref[...] = vr[:, :vw].astype(BF16)
    rb_ref[...] = vr[:, vw:].astype(BF16)
    z_ref[...] = _dot(xn, wz_ref[...])


def in_proj_ab(h, g, cos, sin, w_in, B, S):
    T, D = h.shape
    aw, kw, vw = A_HEADS * A_HEAD_DIM, B_HEADS * B_DK, B_HEADS * B_DV
    o = np.cumsum([0, aw, aw, aw, kw, kw, vw, vw, 2 * B_RANK])
    wb = w_in.astype(BF16)
    wqk, wva, wqkb, wvr, wz = (wb[:, o[0]:o[2]], wb[:, o[2]:o[3]], wb[:, o[3]:o[5]],
                               wb[:, o[5]:o[7]], wb[:, o[7]:o[8]])
    tm = ROW_TILE
    tps = S // tm
    outs = [(aw, BF16), (aw, BF16), (aw, BF16), (kw, F32), (kw, F32), (vw, BF16), (vw, BF16),
            (2 * B_RANK, F32)]
    res_spec = pl.BlockSpec((1, A_FAR_DIL, tm // A_FAR_DIL, aw), lambda i: (i // tps, 0, i % tps, 0))
    res_shape = jax.ShapeDtypeStruct((B, A_FAR_DIL, S // A_FAR_DIL, aw), BF16)
    return pl.pallas_call(
        _in_ab_kernel,
        grid=(T // tm,),
        in_specs=[_row_spec(tm, D), _const_spec((1, D)), _row_spec(tm, LANES), _row_spec(tm, LANES),
                  _const_spec(wqk.shape), _const_spec(wva.shape), _const_spec(wqkb.shape),
                  _const_spec(wvr.shape), _const_spec(wz.shape), _const_spec((tm, tm))],
        out_specs=[_row_spec(tm, w) for w, _ in outs] + [res_spec] * 3,
        out_shape=[jax.ShapeDtypeStruct((T, w), dt) for w, dt in outs] + [res_shape] * 3,
        compiler_params=_params("parallel"),
        name="in_proj_ab",
    )(h, g[None, :], cos, sin, wqk, wva, wqkb, wvr, wz, _residue_permutation(tm))


def _near_windows(S):
    win = ATTN_TQ + 2 * A_NEAR_REACH
    return win, [min(max(t0 - A_NEAR_REACH, 0), S - win) for t0 in range(0, S, ATTN_TQ)]


def _attn_kernel(q_ref, k_ref, v_ref, qr_ref, kr_ref, vr_ref, nbias_ref, fbias_ref, o_ref,
                 of_scr, lf_scr):
    S = q_ref.shape[1]
    lane = lax.broadcasted_iota(jnp.int32, (1, LANES), 1)
    first_head = lane < A_HEAD_DIM

    def softmax_pv(q, k, v, bias):
        outs, lses = [], []
        for hh in range(LANES // A_HEAD_DIM):
            qh = jnp.where((lane // A_HEAD_DIM) == hh, q, jnp.zeros_like(q))
            s = jnp.einsum('rqd,rkd->rqk', qh, k, preferred_element_type=F32) + bias
            mx = jnp.max(s, axis=-1, keepdims=True)
            p = jnp.exp2(s - mx)
            den = jnp.sum(p, axis=-1, keepdims=True)
            outs.append(jnp.einsum('rqk,rkd->rqd', p.astype(BF16), v, preferred_element_type=F32) / den)
            lses.append(mx + jnp.log2(den))
        return jnp.where(first_head, outs[0], outs[1]), jnp.where(first_head, lses[0], lses[1])

    out_r, lse_r = softmax_pv(qr_ref[0], kr_ref[0], vr_ref[0], fbias_ref[...])
    for r in range(A_FAR_DIL):
        rows = pl.ds(r, S // A_FAR_DIL, stride=A_FAR_DIL)
        of_scr[rows, :] = out_r[r]
        lf_scr[rows, :] = lse_r[r]

    win, starts = _near_windows(S)
    windows = lambda ref: jnp.stack([ref[0, ws:ws + win, :] for ws in starts], axis=0)
    out_n, lse_n = softmax_pv(q_ref[0].reshape(len(starts), ATTN_TQ, LANES), windows(k_ref),
                              windows(v_ref), nbias_ref[...])
    out_n, lse_n = out_n.reshape(S, LANES), lse_n.reshape(S, LANES)

    out_f, lse_f = of_scr[...], lf_scr[...]
    mx = jnp.maximum(lse_n, lse_f)
    en, ef = jnp.exp2(lse_n - mx), jnp.exp2(lse_f - mx)
    o_ref[0] = ((en * out_n + ef * out_f) / (en + ef)).astype(o_ref.dtype)


def _near_bias_kernel(o_ref, *, starts):
    _, tq, win = o_ref.shape
    rel = (lax.broadcasted_iota(jnp.int32, (tq, win), 1)
           - lax.broadcasted_iota(jnp.int32, (tq, win), 0))
    for t, ws in enumerate(starts):
        d = rel + (ws - t * tq)
        dist = jnp.abs(d)
        count = jnp.zeros((tq, win), F32)
        for window, dil in A_PATTERNS[:-1]:
            hit = jnp.where((d & (dil - 1)) == 0, dist, 2 * A_NEAR_REACH + 1) <= window // 2
            count = count + jnp.where(hit, 1.0, 0.0)
        o_ref[t] = jnp.where(count > 0.5, jnp.log2(jnp.maximum(count, 1.0)), NEG_BIG)


def _far_bias_kernel(o_ref):
    n = o_ref.shape[0]
    d = lax.broadcasted_iota(jnp.int32, (n, n), 1) - lax.broadcasted_iota(jnp.int32, (n, n), 0)
    window, dil = A_PATTERNS[-1]
    o_ref[...] = jnp.where(jnp.abs(d) <= window // (2 * dil), 0.0, NEG_BIG)


def _attention_biases(S):
    win, starts = _near_windows(S)
    near = pl.pallas_call(
        functools.partial(_near_bias_kernel, starts=tuple(starts)),
        out_shape=jax.ShapeDtypeStruct((len(starts), ATTN_TQ, win), F32),
        name="near_bias",
    )()
    n = S // A_FAR_DIL
    far = pl.pallas_call(_far_bias_kernel, out_shape=jax.ShapeDtypeStruct((n, n), F32),
                         name="far_bias")()
    return near, far


def dilated_attention(qa, ka, va, qr, kr, vr, B, S):
    W = qa.shape[-1]
    q3, k3, v3 = (t.reshape(B, S, W) for t in (qa, ka, va))
    near, far = _attention_biases(S)
    seq = pl.BlockSpec((1, S, LANES), lambda b, p: (b, 0, p))
    res = pl.BlockSpec((1, A_FAR_DIL, S // A_FAR_DIL, LANES), lambda b, p: (b, 0, 0, p))
    out = pl.pallas_call(
        _attn_kernel,
        grid=(B, W // LANES),
        in_specs=[seq, seq, seq, res, res, res, _const_spec(near.shape), _const_spec(far.shape)],
        out_specs=seq,
        out_shape=jax.ShapeDtypeStruct((B, S, W), BF16),
        scratch_shapes=[pltpu.VMEM((S, LANES), F32)] * 2,
        compiler_params=_params("parallel", "parallel"),
        name="dilated_attention",
    )(q3, k3, v3, qr, kr, vr, near, far)
    return out.reshape(B * S, W)


def _log_sigmoid(x):
    return jnp.minimum(x, 0.0) - jnp.log1p(jnp.exp(-jnp.abs(x)))


def _split_bf16(x):
    hi = x.astype(BF16)
    return hi, (x - hi.astype(F32)).astype(BF16)


def _gla_kernel(q_ref, k_ref, v_ref, r_ref, z_ref, wgf_ref, wgb_ref, bgf_ref, bgb_ref, gn_ref,
                ltri_ref, utri_ref, o_ref, gf_scr, gb_scr, upd_scr, st_scr):
    S = q_ref.shape[1]
    C = B_CHUNK
    n_chunks = S // C
    kw, vw = q_ref.shape[2], v_ref.shape[2]
    ct = ltri_ref.shape[0]

    zb = z_ref[0].astype(BF16)
    for w_ref, b_ref, tri_ref, g_scr in ((wgf_ref, bgf_ref, ltri_ref, gf_scr),
                                         (wgb_ref, bgb_ref, utri_ref, gb_scr)):
        lg = _log_sigmoid(_dot(zb, w_ref[0]) + b_ref[0]) * (1.0 / B_TAU)
        hi_lo = jnp.concatenate(_split_bf16(lg), axis=1)
        for t in range(S // ct):
            rows = slice(t * ct, (t + 1) * ct)
            both = _dot(tri_ref[...], hi_lo[rows])
            g_scr[rows, :] = both[:, :kw] + both[:, kw:]

    row_v = lax.broadcasted_iota(jnp.int32, (vw, kw), 0)
    lane_k = lax.broadcasted_iota(jnp.int32, (vw, kw), 1)
    state_mask = (row_v // B_DV) == (lane_k // B_DK)
    lane1 = lax.broadcasted_iota(jnp.int32, (1, kw), 1)
    head_lane = [(lane1 // B_DK) == hh for hh in range(kw // B_DK)]
    col_v = lax.broadcasted_iota(jnp.int32, (1, vw), 1)
    head_col = [(col_v // B_DV) == hh for hh in range(vw // B_DV)]
    qi = lax.broadcasted_iota(jnp.int32, (C, kw), 0)
    kj = lax.broadcasted_iota(jnp.int32, (C, kw), 1) % C
    causal = kj <= qi

    state_mask2 = jnp.concatenate([state_mask, state_mask], axis=1)

    def chunk_rows(n):
        return pl.ds(pl.multiple_of(n * C, C), C)

    def increments(i, carry):
        nu = GLA_UNROLL
        rows = pl.ds(pl.multiple_of(i * (nu * C), nu * C), nu * C)
        chunks = lambda t: t.reshape(nu, C, t.shape[-1])
        gf, gb, k = chunks(gf_scr[rows, :]), chunks(gb_scr[rows, :]), chunks(k_ref[0, rows, :])
        kdec = jnp.concatenate([k * jnp.exp(gf[:, C - 1:C, :] - gf), k * jnp.exp(gb[:, 0:1, :] - gb)],
                               axis=2).astype(BF16)
        upd = jnp.einsum('ucv,uck->uvk', chunks(v_ref[0, rows, :]), kdec, preferred_element_type=F32)
        upd_scr[pl.ds(i * nu, nu)] = jnp.where(state_mask2, upd, 0.0)
        return carry

    lax.fori_loop(0, n_chunks // GLA_UNROLL, increments, 0)

    def recur(i, carry):
        sf, sb = carry
        nb = n_chunks - 1 - i
        st_scr[i, :, 0:kw] = sf.astype(BF16)
        st_scr[nb, :, kw:2 * kw] = sb.astype(BF16)
        af = jnp.exp(gf_scr[pl.ds(i * C + C - 1, 1), :])
        ab = jnp.exp(gb_scr[pl.ds(nb * C, 1), :])
        return af * sf + upd_scr[i, :, 0:kw], ab * sb + upd_scr[nb, :, kw:2 * kw]

    zero = jnp.zeros((vw, kw), F32)
    lax.fori_loop(0, n_chunks, recur, (zero, zero))

    U = GLA_BATCH

    def per_head_rows(t):
        return jnp.concatenate([jnp.where(m, t, jnp.zeros_like(t)) for m in head_lane], axis=1)

    def bdot_nt(a, b):
        return jnp.einsum('umk,unk->umn', a, b, preferred_element_type=F32)

    def outputs(i, carry):
        rows = pl.ds(pl.multiple_of(i * (U * C), U * C), U * C)
        chunks = lambda t: t.reshape(U, C, t.shape[-1])
        gf, gb = chunks(gf_scr[rows, :]), chunks(gb_scr[rows, :])
        q, k, v = chunks(q_ref[0, rows, :]), chunks(k_ref[0, rows, :]), chunks(v_ref[0, rows, :])
        qfb = jnp.concatenate([q * jnp.exp(gf), q * jnp.exp(gb)], axis=2).astype(BF16)
        kf, kb = (k * jnp.exp(-gf)).astype(BF16), (k * jnp.exp(-gb)).astype(BF16)
        att = jnp.where(causal, bdot_nt(qfb[:, :, :kw], per_head_rows(kf)),
                        bdot_nt(qfb[:, :, kw:], per_head_rows(kb)))
        v_heads = jnp.concatenate([jnp.where(m, v, jnp.zeros_like(v)) for m in head_col], axis=1)
        o = bdot_nt(qfb, st_scr[pl.ds(i * U, U)]) + jnp.einsum(
            'umk,ukn->umn', att.astype(BF16), v_heads, preferred_element_type=F32)
        normed = []
        for hh in range(vw // B_DV):
            oh = o[:, :, hh * B_DV:(hh + 1) * B_DV]
            normed.append(oh * lax.rsqrt(jnp.mean(oh * oh, axis=-1, keepdims=True) + EPS))
        gate = _silu(r_ref[0, rows, :].astype(F32))
        res = jnp.concatenate(normed, axis=2).reshape(U * C, vw) * gn_ref[0] * gate
        o_ref[0, rows, :] = res.astype(o_ref.dtype)
        return carry

    lax.fori_loop(0, n_chunks // U, outputs, 0)


def _chunk_tri(ct, chunk, upper):
    i = np.arange(ct)[:, None]
    j = np.arange(ct)[None, :]
    same = (i // chunk) == (j // chunk)
    return jnp.asarray(same & ((j >= i) if upper else (j <= i)), BF16)


def gla(qb, kb, vb, rb, z, wg2, bg, g_norm, B, S):
    npair = B_HEADS // 2
    kw, vw = 2 * B_DK, 2 * B_DV
    q3, k3 = qb.reshape(B, S, npair * kw), kb.reshape(B, S, npair * kw)
    v3, r3 = vb.reshape(B, S, npair * vw), rb.reshape(B, S, npair * vw)
    z3 = z.reshape(B, S, 2 * B_RANK)
    zero = jnp.zeros((B_RANK, B_HEADS * B_DK), F32)
    wgf = jnp.concatenate([wg2[0], zero], axis=0).astype(BF16)
    wgb = jnp.concatenate([zero, wg2[1]], axis=0).astype(BF16)
    pairs = lambda w: w.reshape(w.shape[0], npair, kw).transpose(1, 0, 2)
    ct = 256
    pair_spec = lambda shape: pl.BlockSpec((1,) + shape, lambda b, p: (p, 0, 0))
    seq_spec = lambda w: pl.BlockSpec((1, S, w), lambda b, p: (b, 0, p))
    out = pl.pallas_call(
        _gla_kernel,
        grid=(B, npair),
        in_specs=[seq_spec(kw), seq_spec(kw), seq_spec(vw), seq_spec(vw),
                  pl.BlockSpec((1, S, 2 * B_RANK), lambda b, p: (b, 0, 0)),
                  pair_spec((2 * B_RANK, kw)), pair_spec((2 * B_RANK, kw)),
                  pair_spec((1, kw)), pair_spec((1, kw)), pair_spec((1, vw)),
                  _const_spec((ct, ct)), _const_spec((ct, ct))],
        out_specs=seq_spec(vw),
        out_shape=jax.ShapeDtypeStruct((B, S, npair * vw), BF16),
        scratch_shapes=[pltpu.VMEM((S, kw), F32), pltpu.VMEM((S, kw), F32),
                        pltpu.VMEM((S // B_CHUNK, vw, 2 * kw), F32),
                        pltpu.VMEM((S // B_CHUNK, vw, 2 * kw), BF16)],
        compiler_params=_params("parallel", "parallel"),
        name="gla",
    )(q3, k3, v3, r3, z3, pairs(wgf), pairs(wgb), pairs(bg[0][None, :]), pairs(bg[1][None, :]),
      g_norm.reshape(npair, 1, vw), _chunk_tri(ct, B_CHUNK, False), _chunk_tri(ct, B_CHUNK, True))
    return out.reshape(B * S, npair * vw)


def _atom_transpose(x):
    rows, width = x.shape
    r = lax.broadcasted_iota(jnp.int32, (rows, width), 0)
    a = lax.broadcasted_iota(jnp.int32, (rows, width), 1) // C_GROUP
    for s in range(3):
        d = 1 << s
        rbit = (r & d) != 0
        abit = (a & d) != 0
        partner_row = jnp.where(rbit, pltpu.roll(x, d, axis=0), pltpu.roll(x, rows - d, axis=0))
        moved = jnp.where(abit, pltpu.roll(partner_row, C_GROUP * d, axis=1),
                          pltpu.roll(partner_row, width - C_GROUP * d, axis=1))
        x = jnp.where(rbit == abit, x, moved)
    return x


def _pack_groups(u, x_ref, z_scr):
    rows, width = u.shape
    n_chunks, B = x_ref.shape[1], x_ref.shape[2]
    R = rows // B
    z = _atom_transpose(u)
    for k in range(width // LANES):
        z_scr[k] = z[:, LANES * k:LANES * (k + 1)]
    for k in range(width // LANES):
        for g8 in range(SUBLANES):
            for c in range(n_chunks):
                lo = z_scr[k, pl.ds(c * S5_CHUNK + g8, B, stride=R), :]
                hi = z_scr[k, pl.ds(c * S5_CHUNK + SUBLANES + g8, B, stride=R), :]
                x_ref[SUBLANES * k + g8, c, :, :] = jnp.concatenate([lo, hi], axis=1).astype(x_ref.dtype)


def _unpack_groups(y_ref, z_scr):
    n_tiles, tm, _ = z_scr.shape
    nch = tm // S5_CHUNK
    for k in range(n_tiles):
        for g8 in range(SUBLANES):
            y = y_ref[SUBLANES * k + g8, 0, :, :]
            z_scr[k, pl.ds(g8, nch, stride=S5_CHUNK), :] = y[:, :LANES]
            z_scr[k, pl.ds(SUBLANES + g8, nch, stride=S5_CHUNK), :] = y[:, LANES:]
    return _atom_transpose(jnp.concatenate([z_scr[k] for k in range(n_tiles)], axis=1))


def _in_cd_kernel(x_ref, g_ref, wu_ref, wval_ref, wgate_ref, u_ref, gd_ref, xg_ref, z_scr):
    B, R, D = x_ref.shape
    xn = _rmsnorm(x_ref[...].reshape(B * R, D), g_ref[...]).astype(BF16)
    u = _dot(xn, wu_ref[...])
    u_ref[...] = u.reshape(u_ref.shape)
    _pack_groups(u, xg_ref, z_scr)
    gd = _dot(xn, wval_ref[...]) * _sigmoid(_dot(xn, wgate_ref[...]))
    gd_ref[...] = gd.reshape(gd_ref.shape)


def in_proj_cd(h, g, w_in, B, S):
    T, D = h.shape
    cw = C_GROUP * C_NGROUPS
    dw = (w_in.shape[1] - cw) // 2
    wb = w_in.astype(BF16)
    R = ROW_TILE // B
    xw = S5_CHUNK * C_GROUP
    slab = lambda w: pl.BlockSpec((B, R, w), lambda i: (0, i, 0))
    u, gd, xg = pl.pallas_call(
        _in_cd_kernel,
        grid=(S // R,),
        in_specs=[slab(D), _const_spec((1, D)), _const_spec((D, cw)), _const_spec((D, dw)),
                  _const_spec((D, dw))],
        out_specs=[slab(cw), slab(dw),
                   pl.BlockSpec((C_NGROUPS, R // S5_CHUNK, B, xw), lambda i: (0, i, 0, 0))],
        out_shape=[jax.ShapeDtypeStruct((B, S, cw), F32), jax.ShapeDtypeStruct((B, S, dw), F32),
                   jax.ShapeDtypeStruct((C_NGROUPS, S // S5_CHUNK, B, xw), BF16)],
        scratch_shapes=[pltpu.VMEM((cw // LANES, B * R, LANES), F32)],
        compiler_params=_params("parallel"),
        name="in_proj_cd",
    )(h.reshape(B, S, D), g[None, :], wb[:, :cw], wb[:, cw:cw + dw], wb[:, cw + dw:])
    return u.reshape(T, cw), gd.reshape(T, dw), xg


def _s5_operators(lam_re, lam_im, log_dt, b_re, b_im, c_re, c_im):
    hp = lax.Precision.DEFAULT
    L = S5_CHUNK
    tau = jnp.arange(L + 1, dtype=F32)
    ks, vs, ws, aL = [], [], [], []
    for d in range(2):
        lr, li = lam_re[d], lam_im[d]
        dt = jnp.exp(log_dt[d])[:, None]
        mag = jnp.exp(lr * dt)
        ab_re, ab_im = mag * jnp.cos(li * dt), mag * jnp.sin(li * dt)
        den = lr * lr + li * li
        nr = ab_re - 1.0
        f_re = (nr * lr + ab_im * li) / den
        f_im = (ab_im * lr - nr * li) / den
        bb_re = f_re[..., None] * b_re[d] - f_im[..., None] * b_im[d]
        bb_im = f_re[..., None] * b_im[d] + f_im[..., None] * b_re[d]
        pmag = jnp.exp(lr[None] * dt[None] * tau[:, None, None])
        pr = pmag * jnp.cos(li[None] * dt[None] * tau[:, None, None])
        pi = pmag * jnp.sin(li[None] * dt[None] * tau[:, None, None])
        ca_re = c_re[d][None] * pr[:, :, None, :] - c_im[d][None] * pi[:, :, None, :]
        ca_im = c_re[d][None] * pi[:, :, None, :] + c_im[d][None] * pr[:, :, None, :]
        k = (jnp.einsum('tghp,gpk->gkth', ca_re[:L], bb_re, precision=hp)
             - jnp.einsum('tghp,gpk->gkth', ca_im[:L], bb_im, precision=hp))
        ab_pow_re = pr[:, :, :, None] * bb_re[None] - pi[:, :, :, None] * bb_im[None]
        ab_pow_im = pr[:, :, :, None] * bb_im[None] + pi[:, :, :, None] * bb_re[None]
        order_v = (lambda t: t[L - 1::-1]) if d == 0 else (lambda t: t[:L])
        order_w = (lambda t: t[1:L + 1]) if d == 0 else (lambda t: t[L:0:-1])
        v = jnp.concatenate([order_v(ab_pow_re), order_v(ab_pow_im)], axis=2)
        vs.append(v.transpose(1, 0, 3, 2).reshape(v.shape[1], L * C_GROUP, 2 * C_STATE))
        w = jnp.concatenate([order_w(ca_re), -order_w(ca_im)], axis=3)
        ws.append(w.transpose(1, 3, 0, 2).reshape(w.shape[1], 2 * C_STATE, L * C_GROUP))
        ks.append(k)
        aL.append(jnp.stack([pr[L], pi[L]], axis=0))
    kf, kb = ks
    by_lag = jnp.concatenate([kb[:, :, :0:-1], kf[:, :, :1] + kb[:, :, :1], kf[:, :, 1:]], axis=2)
    n = by_lag.shape[0]
    m = jnp.stack([by_lag[:, :, L - 1 - j:2 * L - 1 - j].reshape(n, C_GROUP, L * C_GROUP)
                   for j in range(L)], axis=1)
    m = m.reshape(n, L * C_GROUP, L * C_GROUP)
    return m, vs[0], vs[1], ws[0], ws[1], jnp.stack(aL, axis=0)


def _pair_blockdiag(t):
    G, r, c = t.shape
    t = t.reshape(G // 2, 2, r, c)
    z = jnp.zeros((G // 2, r, c), t.dtype)
    return jnp.concatenate([jnp.concatenate([t[:, 0], z], axis=2),
                            jnp.concatenate([z, t[:, 1]], axis=2)], axis=1)


def _s5_pair_operators(params):
    fold = lambda t: jnp.moveaxis(t, 1, 0).reshape((2, t.shape[0] * t.shape[2]) + t.shape[3:])
    m, vf, vb, wf, wb, aL = _s5_operators(*(fold(t) for t in params))
    P = C_STATE
    mm = _pair_blockdiag(m)
    vcols = [_pair_blockdiag(v[:, :, s]) for v in (vf, vb) for s in (slice(0, P), slice(P, 2 * P))]
    vv = jnp.concatenate(vcols, axis=2)
    wrows = [_pair_blockdiag(w[:, s, :]) for w in (wf, wb) for s in (slice(0, P), slice(P, 2 * P))]
    ww = jnp.concatenate(wrows, axis=1)
    G = aL.shape[2]
    aa = aL.reshape(4, G // 2, 2 * P).transpose(1, 0, 2)
    return mm.astype(BF16), vv.astype(BF16), ww.astype(BF16), aa


def _s5_kernel(x_ref, m_ref, v_ref, w_ref, a_ref, y_ref, v_scr, s_scr):
    _, N, B, half = x_ref.shape
    lw = a_ref.shape[2]
    x = jnp.concatenate([x_ref[0].reshape(N * B, half), x_ref[1].reshape(N * B, half)], axis=1)
    v = _dot(x, v_ref[0])
    for p in range(4):
        v_scr[p] = v[:, lw * p:lw * (p + 1)]
    a = a_ref[0]
    afr, afi, abr, abi = a[0:1], a[1:2], a[2:3], a[3:4]

    def step(n, carry):
        fr, fi, br, bi = carry
        rf = pl.ds(pl.multiple_of(n * B, B), B)
        rb = pl.ds(pl.multiple_of((N - 1 - n) * B, B), B)
        s_scr[0, rf, :] = fr
        s_scr[1, rf, :] = fi
        s_scr[2, rb, :] = br
        s_scr[3, rb, :] = bi
        return (afr * fr - afi * fi + v_scr[0, rf, :], afr * fi + afi * fr + v_scr[1, rf, :],
                abr * br - abi * bi + v_scr[2, rb, :], abr * bi + abi * br + v_scr[3, rb, :])

    zero = jnp.zeros((B, lw), F32)
    lax.fori_loop(0, N, step, (zero, zero, zero, zero))
    s = jnp.concatenate([s_scr[p] for p in range(4)], axis=1).astype(BF16)
    y = _dot(x, m_ref[0]) + _dot(s, w_ref[0])
    parts = 2 * half // lw
    for p in range(parts):
        v_scr[p] = y[:, lw * p:lw * (p + 1)]
    for b in range(B):
        for p in range(parts):
            lanes = pl.ds((p * lw) % half, lw)
            y_ref[p * lw // half, b, :, lanes] = v_scr[p, pl.ds(b, N, stride=B), :]


def s5_bidirectional(xg, operators, layer):
    G, N, B, half = xg.shape
    mm, vv, ww, aa = operators
    pw = 2 * half
    sw = vv.shape[2]
    first = layer * (G // 2)
    op_spec = lambda shape: pl.BlockSpec((1,) + shape, lambda p: (first + p, 0, 0))
    pair_spec = lambda d1, d2: pl.BlockSpec((2, d1, d2, half), lambda p: (p, 0, 0, 0))
    return pl.pallas_call(
        _s5_kernel,
        grid=(G // 2,),
        in_specs=[pair_spec(N, B), op_spec((pw, pw)), op_spec((pw, sw)), op_spec((sw, pw)),
                  op_spec((4, sw // 4))],
        out_specs=pair_spec(B, N),
        out_shape=jax.ShapeDtypeStruct((G, B, N, half), F32),
        scratch_shapes=[pltpu.VMEM((4, B * N, sw // 4), F32)] * 2,
        compiler_params=_params("parallel"),
        name="s5",
    )(xg, mm, vv, ww, aa)


def _halo_rows(prev_ref, next_ref, tiles_per_seq):
    t = pl.program_id(0) % tiles_per_seq
    prev = jnp.where(t > 0, prev_ref[...], 0.0)
    nxt = jnp.where(t < tiles_per_seq - 1, next_ref[...], 0.0)
    return prev, nxt


def _cd_out_kernel(y_ref, u_ref, gd_ref, gdp_ref, gdn_ref, h_ref, dskip_ref, wglu_ref, bglu_ref,
                   cw_ref, cb_ref, lng_ref, lnb_ref, wc_ref, wd_ref, o_ref, ext_scr, z_scr, sh_scr, *,
                   tiles_per_seq):
    tm = u_ref.shape[0]
    z = jax.nn.gelu(_unpack_groups(y_ref, z_scr) + dskip_ref[...] * u_ref[...])
    o_c = z * _sigmoid(_dot(z.astype(BF16), wglu_ref[...]) + bglu_ref[...])
    prev, nxt = _halo_rows(gdp_ref, gdn_ref, tiles_per_seq)
    ext_scr[0:CONV_HALO, :] = prev
    ext_scr[CONV_HALO:CONV_HALO + tm, :] = gd_ref[...]
    ext_scr[CONV_HALO + tm:CONV_HALO + tm + CONV_HALO, :] = nxt
    n_rows = tm + 2 * CONV_HALO - SUBLANES
    for r in range(1, SUBLANES):
        sh_scr[r - 1] = ext_scr[r:r + n_rows, :]
    base = CONV_HALO - (D_KERNEL - 1) // 2
    acc = jnp.zeros(gd_ref.shape, F32) + cb_ref[...]
    for kk in range(D_KERNEL):
        q, r = divmod(base + kk, SUBLANES)
        rows = slice(q * SUBLANES, q * SUBLANES + tm)
        tap = ext_scr[rows, :] if r == 0 else sh_scr[r - 1, rows, :]
        acc = acc + cw_ref[kk:kk + 1, :] * tap
    mu = jnp.mean(acc, axis=-1, keepdims=True)
    cen = acc - mu
    var = jnp.mean(cen * cen, axis=-1, keepdims=True)
    o_d = _silu(cen * lax.rsqrt(var + EPS) * lng_ref[...] + lnb_ref[...])
    o_ref[...] = (h_ref[...] + _dot(o_c.astype(BF16), wc_ref[...])
                  + _dot(o_d.astype(BF16), wd_ref[...]))


def cd_out(yg, u, gd, h, d_skip, w_glu, b_glu, conv_w, conv_b, ln_g, ln_b, w_out, S):
    T, D = h.shape
    cw, dw = u.shape[1], gd.shape[1]
    tm = ROW_TILE
    tiles_per_seq = tps = S // tm
    hb = tm // CONV_HALO
    n_halo_blocks = T // CONV_HALO
    row = lambda v: v[None, :]
    return pl.pallas_call(
        functools.partial(_cd_out_kernel, tiles_per_seq=tiles_per_seq),
        grid=(T // tm,),
        in_specs=[pl.BlockSpec((yg.shape[0], 1, tm // S5_CHUNK, yg.shape[3]),
                               lambda i: (0, i // tps, i % tps, 0)),
                  _row_spec(tm, cw), _row_spec(tm, dw),
                  pl.BlockSpec((CONV_HALO, dw), lambda i: (jnp.maximum(i * hb - 1, 0), 0)),
                  pl.BlockSpec((CONV_HALO, dw), lambda i: (jnp.minimum((i + 1) * hb, n_halo_blocks - 1), 0)),
                  _row_spec(tm, D), _const_spec((1, cw)), _const_spec((cw, cw)), _const_spec((1, cw)),
                  _const_spec((D_KERNEL, dw)), _const_spec((1, dw)), _const_spec((1, dw)),
                  _const_spec((1, dw)), _const_spec((cw, D)), _const_spec((dw, D))],
        out_specs=_row_spec(tm, D),
        out_shape=jax.ShapeDtypeStruct((T, D), F32),
        scratch_shapes=[pltpu.VMEM((tm + 2 * CONV_HALO, dw), F32), pltpu.VMEM((cw // LANES, tm, LANES), F32),
                        pltpu.VMEM((SUBLANES - 1, tm + 2 * CONV_HALO - SUBLANES, dw), F32)],
        compiler_params=_params("parallel"),
        name="cd_out",
    )(yg, u, gd, gd, gd, h, row(d_skip), w_glu.astype(BF16), row(b_glu), conv_w, row(conv_b),
      row(ln_g), row(ln_b), w_out[:cw].astype(BF16), w_out[cw:].astype(BF16))


def _kv_kernel(m_ref, g_ref, w_ref, o_ref):
    o_ref[...] = _dot(_rmsnorm(m_ref[...], g_ref[...]).astype(BF16), w_ref[0]).astype(o_ref.dtype)


def mem_kv(mem2, g, wkv, layer):
    T, D = mem2.shape
    tm = ROW_TILE
    width = wkv.shape[2]
    return pl.pallas_call(
        _kv_kernel,
        grid=(T // tm,),
        in_specs=[_row_spec(tm, D), _const_spec((1, D)), _layer_spec((D, width), layer)],
        out_specs=_row_spec(tm, width),
        out_shape=jax.ShapeDtypeStruct((T, width), BF16),
        compiler_params=_params("parallel"),
        name="mem_kv",
    )(mem2, g[None, :], wkv)


def _xattn_kernel(x_ref, g_ref, wq_ref, k_ref, v_ref, wo_ref, *rest, mixer_out):
    if mixer_out:
        a_ref, b_ref, wa_ref, wb_ref, o_ref = rest
        x = x_ref[...] + _dot(a_ref[...], wa_ref[...]) + _dot(b_ref[...], wb_ref[...])
    else:
        (o_ref,) = rest
        x = x_ref[...]
    hd = x.shape[1] // X_HEADS
    q = (_dot(_rmsnorm(x, g_ref[...]).astype(BF16), wq_ref[0]) * (hd ** -0.5)).astype(BF16)
    heads = []
    for hh in range(X_HEADS):
        cols = slice(hh * hd, (hh + 1) * hd)
        s = _dot_nt(q[:, cols], k_ref[0, :, cols])
        p = jnp.exp(s - jnp.max(s, axis=-1, keepdims=True))
        den = jnp.sum(p, axis=-1, keepdims=True)
        heads.append((_dot(p.astype(BF16), v_ref[0, :, cols]) / den).astype(BF16))
    o_ref[...] = x + _dot(jnp.concatenate(heads, axis=1), wo_ref[0])


def cross_attention(h, g, wq, kv, wo, B, S, layer, mixer=None):
    T, D = h.shape
    M = kv.shape[0] // B
    kv3 = kv.reshape(B, M, 2 * D)
    tm = WIDE_ROW_TILE
    tps = S // tm
    in_specs = [_row_spec(tm, D), _const_spec((1, D)), _layer_spec((D, D), layer),
                pl.BlockSpec((1, M, D), lambda i: (i // tps, 0, 0)),
                pl.BlockSpec((1, M, D), lambda i: (i // tps, 0, 1)),
                _layer_spec((D, D), layer)]
    args = [h, g[None, :], wq, kv3, kv3, wo]
    if mixer is not None:
        a, b, w_out = mixer
        wa, wb = w_out[:a.shape[1]].astype(BF16), w_out[a.shape[1]:].astype(BF16)
        in_specs += [_row_spec(tm, a.shape[1]), _row_spec(tm, b.shape[1]), _const_spec(wa.shape),
                     _const_spec(wb.shape)]
        args += [a, b, wa, wb]
    return pl.pallas_call(
        functools.partial(_xattn_kernel, mixer_out=mixer is not None),
        grid=(T // tm,),
        in_specs=in_specs,
        out_specs=_row_spec(tm, D),
        out_shape=jax.ShapeDtypeStruct((T, D), F32),
        compiler_params=_params("parallel"),
        name="cross_attention",
    )(*args)


def _ffn_kernel(x_ref, xp_ref, xn_ref, g_ref, wv_ref, wg_ref, cwv_ref, cwg_ref, cbv_ref, cbg_ref,
                wd_ref, gfin_ref, o_ref, act_scr, *, tiles_per_seq, n_split, final_norm):
    tm = x_ref.shape[0]
    x = x_ref[...]
    prev, nxt = _halo_rows(xp_ref, xn_ref, tiles_per_seq)
    xe = _rmsnorm(jnp.concatenate([prev, x, nxt], axis=0), g_ref[...]).astype(BF16)
    rows_ext = xe.shape[0]
    fw = wv_ref.shape[2] // n_split
    mid = slice(FFN_HALO, FFN_HALO + tm)

    def conv(u, cw_ref, cb_ref, cols):
        before = pltpu.roll(u, 1, axis=0)[mid]
        after = pltpu.roll(u, rows_ext - 1, axis=0)[mid]
        return (cb_ref[:, cols] + cw_ref[0:1, cols] * before + cw_ref[1:2, cols] * u[mid]
                + cw_ref[2:3, cols] * after)

    for c in range(n_split):
        cols = slice(c * fw, (c + 1) * fw)
        gate = conv(_dot(xe, wg_ref[0, :, cols]), cwg_ref, cbg_ref, cols)
        val = conv(_dot(xe, wv_ref[0, :, cols]), cwv_ref, cbv_ref, cols)
        act_scr[:, cols] = (_silu(gate) * val).astype(BF16)
    acc = x + _dot(act_scr[...], wd_ref[0])
    if final_norm:
        acc = _rmsnorm(acc, gfin_ref[...])
    o_ref[...] = acc


def conv_ffn(h, g, w_up, w_conv, b_conv, w_down, g_final, S, layer, final_norm):
    T, D = h.shape
    F = w_down.shape[1]
    tm = ROW_TILE
    tiles_per_seq = S // tm
    hb = tm // FFN_HALO
    n_halo_blocks = T // FFN_HALO
    n_split = F // FFN_COLS
    row = lambda v: v[None, :]
    return pl.pallas_call(
        functools.partial(_ffn_kernel, tiles_per_seq=tiles_per_seq, n_split=n_split,
                          final_norm=final_norm),
        grid=(T // tm,),
        in_specs=[_row_spec(tm, D),
                  pl.BlockSpec((FFN_HALO, D), lambda i: (jnp.maximum(i * hb - 1, 0), 0)),
                  pl.BlockSpec((FFN_HALO, D), lambda i: (jnp.minimum((i + 1) * hb, n_halo_blocks - 1), 0)),
                  _const_spec((1, D)), _layer_spec((D, F), layer, 0), _layer_spec((D, F), layer, 1),
                  _const_spec((FFN_KERNEL, F)), _const_spec((FFN_KERNEL, F)),
                  _const_spec((1, F)), _const_spec((1, F)), _layer_spec((F, D), layer),
                  _const_spec((1, D))],
        out_specs=_row_spec(tm, D),
        out_shape=jax.ShapeDtypeStruct((T, D), F32),
        scratch_shapes=[pltpu.VMEM((tm, F), BF16)],
        compiler_params=_params("parallel"),
        name="conv_ffn",
    )(h, h, h, row(g), w_up, w_up, w_conv[:, :F], w_conv[:, F:], row(b_conv[:F]),
      row(b_conv[F:]), w_down, row(g_final))


def kernel(x, mem, positions, g_mix, g_xattn, g_mem, w_xq, w_xkv, w_xo, g_ffn, w_up, w_conv_ffn,
           b_conv_ffn, w_down, w_in_ab, w_out_ab, gla_wg2, gla_bg, gla_norm, w_in_cd, w_out_cd,
           s5_lam_re, s5_lam_im, s5_log_dt, s5_b_re, s5_b_im, s5_c_re, s5_c_im, s5_d, s5_w_glu,
           s5_b_glu, conv_w, conv_b, conv_ln_g, conv_ln_b, g_final):
    B, S, D = x.shape
    depth = g_mix.shape[0]
    assert S % ROW_TILE == 0 and S % B_CHUNK == 0 and S % S5_CHUNK == 0
    h = x.reshape(B * S, D)
    mem2 = mem.reshape(-1, D)
    cos, sin = rope_tables(positions)
    s5_ops = _s5_pair_operators((s5_lam_re, s5_lam_im, s5_log_dt, s5_b_re, s5_b_im, s5_c_re, s5_c_im))
    w_xq, w_xkv, w_xo, w_up, w_down = (w.astype(BF16) for w in (w_xq, w_xkv, w_xo, w_up, w_down))
    for layer in range(depth):
        i = layer // 2
        if layer % 2 == 0:
            qa, ka, va, qb, kb, vb, rb, z, qr, kr, vr = in_proj_ab(h, g_mix[layer], cos, sin,
                                                                   w_in_ab[i], B, S)
            o_a = dilated_attention(qa, ka, va, qr, kr, vr, B, S)
            o_b = gla(qb, kb, vb, rb, z, gla_wg2[i], gla_bg[i], gla_norm[i], B, S)
            mixer = (o_a, o_b, w_out_ab[i])
        else:
            mixer = None
            u, gd, xg = in_proj_cd(h, g_mix[layer], w_in_cd[i], B, S)
            yg = s5_bidirectional(xg, s5_ops, i)
            h = cd_out(yg, u, gd, h, s5_d[i], s5_w_glu[i], s5_b_glu[i], conv_w[i], conv_b[i],
                       conv_ln_g[i], conv_ln_b[i], w_out_cd[i], S)
        kv = mem_kv(mem2, g_mem[layer], w_xkv, layer)
        h = cross_attention(h, g_xattn[layer], w_xq, kv, w_xo, B, S, layer, mixer)
        h = conv_ffn(h, g_ffn[layer], w_up, w_conv_ffn[layer], b_conv_ffn[layer], w_down, g_final,
                     S, layer, final_norm=(layer == depth - 1))
    return h.reshape(B, S, D)
```
